```python
import math
import jax, jax.numpy as jnp
from jax import lax
import numpy as np

D_MODEL = 2048
BATCH = 8
SEQ = 2048
DEPTH = 4

N_MIXERS = 3
BLK = 128
EPS = 1e-6
HEAD_DIM = 64
D_FF = 4 * D_MODEL

SWA_HEADS = D_MODEL // HEAD_DIM
SWA_KV = SWA_HEADS // 8
SWA_WINDOW = 128
SWA_IN = (SWA_HEADS + 2 * SWA_KV) * HEAD_DIM

DSA_HD = 128
DSA_HEADS = D_MODEL // DSA_HD
DSA_QRANK = D_MODEL // 4
DSA_KVRANK = D_MODEL // 8
IDX_HEADS = 16
IDX_DIM = 64
TOPK_MAX = 256
DSA_IN = DSA_QRANK + DSA_KVRANK + IDX_DIM + IDX_HEADS

DIL_PATTERNS = ((128, 1), (512, 4), (2048, 16))
DIL_GROUPS = len(DIL_PATTERNS)
DIL_HEADS = D_MODEL // 128
DIL_KV = DIL_HEADS // 4
DIL_GROUP_COLS = (DIL_HEADS + 2 * DIL_KV) * HEAD_DIM
DIL_IN = DIL_GROUPS * DIL_GROUP_COLS
DIL_PAD = max(w for w, _ in DIL_PATTERNS)

N_SWA = (DEPTH + 2) // 3
N_DSA = (DEPTH + 1) // 3
N_DIL = DEPTH // 3

kernel_name = "hybrid_swa_dsa_dilated_adaln_trunk"


def rms_norm(x, g):
    xf = x.astype(jnp.float32)
    y = xf * lax.rsqrt(jnp.mean(xf * xf, axis=-1, keepdims=True) + EPS)
    return (y * g.astype(jnp.float32)).astype(x.dtype)


def alibi_slopes(n):
    return 2.0 ** (-8.0 * jnp.arange(1, n + 1, dtype=jnp.float32) / n)


def sweep_blocks(fn, seq_len):
    out = lax.map(fn, jnp.arange(seq_len // BLK, dtype=jnp.int32) * BLK)
    out = jnp.moveaxis(out, 0, 1)
    return out.reshape(out.shape[0], seq_len, *out.shape[3:])


def mixer_swa(h, w_in, qn_g, kn_g, sinks, w_out):
    B, S, _ = h.shape
    G = SWA_HEADS // SWA_KV
    q, k, v = jnp.split(h @ w_in, [SWA_HEADS * HEAD_DIM, (SWA_HEADS + SWA_KV) * HEAD_DIM], axis=-1)
    q = rms_norm(q.reshape(B, S, SWA_KV, G, HEAD_DIM), qn_g)
    k = rms_norm(k.reshape(B, S, SWA_KV, HEAD_DIM), kn_g)
    v = v.reshape(B, S, SWA_KV, HEAD_DIM)
    pad = ((0, 0), (SWA_WINDOW, 0), (0, 0), (0, 0))
    k_pad, v_pad = jnp.pad(k, pad), jnp.pad(v, pad)
    slopes = alibi_slopes(SWA_HEADS).reshape(SWA_KV, G)
    sink = sinks.astype(jnp.float32).reshape(SWA_KV, G)
    span = BLK + SWA_WINDOW
    scale = HEAD_DIM ** -0.5

    def block(t0):
        qb = lax.dynamic_slice_in_dim(q, t0, BLK, 1)
        kb = lax.dynamic_slice_in_dim(k_pad, t0, span, 1)
        vb = lax.dynamic_slice_in_dim(v_pad, t0, span, 1)
        t = t0 + jnp.arange(BLK)
        s = t0 - SWA_WINDOW + jnp.arange(span)
        dist = t[:, None] - s[None, :]
        valid = (dist >= 0) & (dist < SWA_WINDOW) & (s[None, :] >= 0)
        logits = (jnp.einsum('bqkgd,bskd->bkgqs', qb, kb).astype(jnp.float32) * scale
                  - slopes[:, :, None, None] * dist.astype(jnp.float32))
        logits = jnp.where(valid, logits, -jnp.inf)
        sink_col = jnp.broadcast_to(sink[None, :, :, None, None], logits.shape[:-1] + (1,))
        p = jax.nn.softmax(jnp.concatenate([sink_col, logits], axis=-1), axis=-1)[..., 1:]
        return jnp.einsum('bkgqs,bskd->bqkgd', p.astype(vb.dtype), vb)

    o = sweep_blocks(block, S)
    return o.reshape(B, S, -1) @ w_out


def mixer_dsa(h, w_in, qlat_g, kvlat_g, w_uq, w_ukv, w_idx_q, qn_g, kn_g, w_out):
    B, S, _ = h.shape
    topk = min(TOPK_MAX, S // 4)
    q_lat, kv_lat, k_idx, w_idx = jnp.split(
        h @ w_in, [DSA_QRANK, DSA_QRANK + DSA_KVRANK, DSA_QRANK + DSA_KVRANK + IDX_DIM], axis=-1)
    q_lat = rms_norm(q_lat, qlat_g)
    kv_lat = rms_norm(kv_lat, kvlat_g)
    q = rms_norm((q_lat @ w_uq).reshape(B, S, DSA_HEADS, DSA_HD), qn_g)
    k, v = jnp.split(kv_lat @ w_ukv, 2, axis=-1)
    k = rms_norm(k, kn_g)
    q_idx = (q_lat @ w_idx_q).reshape(B, S, IDX_HEADS, IDX_DIM)
    w_idx = w_idx.astype(jnp.float32) * (IDX_HEADS ** -0.5)
    slopes = alibi_slopes(DSA_HEADS)
    scale = DSA_HD ** -0.5
    s_all = jnp.arange(S)

    def block(t0):
        t = t0 + jnp.arange(BLK)
        qi = lax.dynamic_slice_in_dim(q_idx, t0, BLK, 1)
        wi = lax.dynamic_slice_in_dim(w_idx, t0, BLK, 1)
        rel = jax.nn.relu(jnp.einsum('bqhd,bsd->bqhs', qi, k_idx).astype(jnp.float32) * IDX_DIM ** -0.5)
        score = jnp.einsum('bqh,bqhs->bqs', wi, rel)
        score = jnp.where(s_all[None, None, :] <= t[None, :, None], score, -jnp.inf)
        _, idx = lax.top_k(score, topk)
        valid = idx <= t[None, :, None]
        k_sel = jax.vmap(lambda kb, ib: kb[ib])(k, idx)
        v_sel = jax.vmap(lambda vb, ib: vb[ib])(v, idx)
        qb = lax.dynamic_slice_in_dim(q, t0, BLK, 1)
        dist = (t[None, :, None] - idx).astype(jnp.float32)
        logits = (jnp.einsum('bqhd,bqjd->bhqj', qb, k_sel).astype(jnp.float32) * scale
                  - slopes[None, :, None, None] * dist[:, None])
        logits = jnp.where(valid[:, None], logits, -jnp.inf)
        p = jax.nn.softmax(logits, axis=-1)
        return jnp.einsum('bhqj,bqjd->bqhd', p.astype(v_sel.dtype), v_sel)

    o = sweep_blocks(block, S)
    return o.reshape(B, S, -1) @ w_out


def mixer_dilated(h, w_in, qn_g, kn_g, w_out):
    B, S, _ = h.shape
    G = DIL_HEADS // DIL_KV
    proj = h @ w_in
    pad = ((0, 0), (DIL_PAD, 0), (0, 0), (0, 0))
    groups = []
    for gi in range(DIL_GROUPS):
        part = proj[..., gi * DIL_GROUP_COLS:(gi + 1) * DIL_GROUP_COLS]
        q, k, v = jnp.split(part, [DIL_HEADS * HEAD_DIM, (DIL_HEADS + DIL_KV) * HEAD_DIM], axis=-1)
        q = rms_norm(q.reshape(B, S, DIL_KV, G, HEAD_DIM), qn_g[gi])
        k = rms_norm(k.reshape(B, S, DIL_KV, HEAD_DIM), kn_g[gi])
        v = v.reshape(B, S, DIL_KV, HEAD_DIM)
        groups.append((q, jnp.pad(k, pad), jnp.pad(v, pad)))
    slopes = alibi_slopes(DIL_HEADS).reshape(DIL_KV, G)
    scale = HEAD_DIM ** -0.5

    def block(t0):
        t = t0 + jnp.arange(BLK)
        outs, lses = [], []
        for (win, dil), (q, k_pad, v_pad) in zip(DIL_PATTERNS, groups):
            j = jnp.arange(win // dil + 1)
            src = t[:, None] - j[None, :] * dil
            valid = src >= 0
            kg = jnp.take(k_pad, src + DIL_PAD, axis=1)
            vg = jnp.take(v_pad, src + DIL_PAD, axis=1)
            qb = lax.dynamic_slice_in_dim(q, t0, BLK, 1)
            logits = (jnp.einsum('bqkgd,bqjkd->bkgqj', qb, kg).astype(jnp.float32) * scale
                      - slopes[:, :, None, None] * (j * dil).astype(jnp.float32))
            logits = jnp.where(valid, logits, -jnp.inf)
            lse = jax.nn.logsumexp(logits, axis=-1)
            p = jnp.exp(logits - lse[..., None])
            outs.append(jnp.einsum('bkgqj,bqjkd->bqkgd', p.astype(vg.dtype), vg))
            lses.append(lse)
        wts = jax.nn.softmax(jnp.stack(lses, 0), axis=0)
        wts = jnp.transpose(wts, (0, 1, 4, 2, 3))
        return jnp.einsum('nbqkg,nbqkgd->bqkgd', wts.astype(outs[0].dtype), jnp.stack(outs, 0))

    o = sweep_blocks(block, S)
    return o.reshape(B, S, -1) @ w_out


def squared_relu_mlp(h, w1, w2):
    return jnp.square(jax.nn.relu(h @ w1)) @ w2


def setup_inputs(seed: int = 0) -> dict:
    key = jax.random.key(seed)
    ks = iter(jax.random.split(key, 40))

    def nrm(shape, scale):
        return jax.random.normal(next(ks), shape, jnp.float32) * scale

    def gain(shape):
        return 1.0 + 0.02 * jax.random.normal(next(ks), shape, jnp.float32)

    D = D_MODEL
    return {
        "x": nrm((BATCH, SEQ, D), 1.0),
        "c": nrm((BATCH, D), 1.0),
        "ada_w": nrm((DEPTH, D, 6 * D), D ** -0.5),
        "ada_b": nrm((DEPTH, 6 * D), 0.02),
        "norm1_g": gain((DEPTH, D)),
        "norm2_g": gain((DEPTH, D)),
        "mlp_w1": nrm((DEPTH, D, D_FF), D ** -0.5),
        "mlp_w2": nrm((DEPTH, D_FF, D), D_FF ** -0.5),
        "swa_w_in": nrm((N_SWA, D, SWA_IN), D ** -0.5),
        "swa_qn_g": gain((N_SWA, HEAD_DIM)),
        "swa_kn_g": gain((N_SWA, HEAD_DIM)),
        "swa_sinks": nrm((N_SWA, SWA_HEADS), 1.0),
        "swa_w_out": nrm((N_SWA, SWA_HEADS * HEAD_DIM, D), (SWA_HEADS * HEAD_DIM) ** -0.5),
        "dsa_w_in": nrm((N_DSA, D, DSA_IN), D ** -0.5),
        "dsa_qlat_g": gain((N_DSA, DSA_QRANK)),
        "dsa_kvlat_g": gain((N_DSA, DSA_KVRANK)),
        "dsa_w_uq": nrm((N_DSA, DSA_QRANK, DSA_HEADS * DSA_HD), DSA_QRANK ** -0.5),
        "dsa_w_ukv": nrm((N_DSA, DSA_KVRANK, 2 * DSA_HD), DSA_KVRANK ** -0.5),
        "dsa_w_idx_q": nrm((N_DSA, DSA_QRANK, IDX_HEADS * IDX_DIM), DSA_QRANK ** -0.5),
        "dsa_qn_g": gain((N_DSA, DSA_HD)),
        "dsa_kn_g": gain((N_DSA, DSA_HD)),
        "dsa_w_out": nrm((N_DSA, DSA_HEADS * DSA_HD, D), (DSA_HEADS * DSA_HD) ** -0.5),
        "dil_w_in": nrm((N_DIL, D, DIL_IN), D ** -0.5),
        "dil_qn_g": gain((N_DIL, DIL_GROUPS, HEAD_DIM)),
        "dil_kn_g": gain((N_DIL, DIL_GROUPS, HEAD_DIM)),
        "dil_w_out": nrm((N_DIL, DIL_HEADS * HEAD_DIM, D), (DIL_HEADS * HEAD_DIM) ** -0.5),
    }


def reference(x, c, ada_w, ada_b, norm1_g, norm2_g, mlp_w1, mlp_w2,
              swa_w_in, swa_qn_g, swa_kn_g, swa_sinks, swa_w_out,
              dsa_w_in, dsa_qlat_g, dsa_kvlat_g, dsa_w_uq, dsa_w_ukv, dsa_w_idx_q,
              dsa_qn_g, dsa_kn_g, dsa_w_out,
              dil_w_in, dil_qn_g, dil_kn_g, dil_w_out):
    cond = jax.nn.silu(c)
    for i in range(DEPTH):
        mod = cond @ ada_w[i] + ada_b[i]
        sh1, sc1, g1, sh2, sc2, g2 = jnp.split(mod[:, None, :], 6, axis=-1)
        h = rms_norm(x, norm1_g[i]) * (1.0 + sc1) + sh1
        kind, j = i % N_MIXERS, i // N_MIXERS
        if kind == 0:
            y = mixer_swa(h, swa_w_in[j], swa_qn_g[j], swa_kn_g[j], swa_sinks[j], swa_w_out[j])
        elif kind == 1:
            y = mixer_dsa(h, dsa_w_in[j], dsa_qlat_g[j], dsa_kvlat_g[j], dsa_w_uq[j], dsa_w_ukv[j],
                          dsa_w_idx_q[j], dsa_qn_g[j], dsa_kn_g[j], dsa_w_out[j])
        else:
            y = mixer_dilated(h, dil_w_in[j], dil_qn_g[j], dil_kn_g[j], dil_w_out[j])
        x = x + g1 * y
        h = rms_norm(x, norm2_g[i]) * (1.0 + sc2) + sh2
        x = x + g2 * squared_relu_mlp(h, mlp_w1[i], mlp_w2[i])
    return x
```

```python
import functools

import jax
import jax.numpy as jnp
import numpy as np
from jax import lax
from jax.experimental import pallas as pl
from jax.experimental.pallas import tpu as pltpu

F32 = jnp.float32
BF = jnp.bfloat16
NT = (((1,), (1,)), ((), ()))

EPS = 1e-6
NEG = -1e30
BLK = 128
HEAD_DIM = 64
TOPK_MAX = 256
DIL_PATTERNS = ((128, 1), (512, 4), (2048, 16))
VMEM_LIMIT = 56 * 1024 * 1024


def _alibi_slopes(n):
    return [float(2.0 ** (-8.0 * (i + 1) / n)) for i in range(n)]


def _cparams(sem):
    return pltpu.CompilerParams(dimension_semantics=sem, vmem_limit_bytes=VMEM_LIMIT)


def _mod_kernel(c_ref, w_ref, b_ref, o_ref):
    c = c_ref[...]
    cond = c * (1.0 / (1.0 + jnp.exp(-c)))
    o_ref[0] = jnp.dot(cond.astype(BF), w_ref[0].astype(BF),
                       preferred_element_type=F32) + b_ref[0]


def _adaln_mod(c, ada_w, ada_b):
    depth, d, n = ada_w.shape
    b = c.shape[0]
    tn = 1024
    return pl.pallas_call(
        _mod_kernel,
        out_shape=jax.ShapeDtypeStruct((depth, b, n), F32),
        grid=(depth, n // tn),
        in_specs=[pl.BlockSpec((b, d), lambda l, j: (0, 0)),
                  pl.BlockSpec((1, d, tn), lambda l, j: (l, 0, j)),
                  pl.BlockSpec((1, 1, tn), lambda l, j: (l, 0, j))],
        out_specs=pl.BlockSpec((1, b, tn), lambda l, j: (l, 0, j)),
        compiler_params=_cparams(("parallel", "parallel")),
        name="adaln_mod",
    )(c, ada_w, ada_b.reshape(depth, 1, n))


def _mm_kernel(*refs, prologue, epilogue, tm):
    refs = list(refs)
    lhs_ref = refs.pop(0)
    if prologue:
        ng_ref, sh_ref, sc_ref = refs.pop(0), refs.pop(0), refs.pop(0)
    w_ref = refs.pop(0)
    if epilogue == "gnorm":
        gm_ref, gain_ref, flag_ref = refs.pop(0), refs.pop(0), refs.pop(0)
    if epilogue == "resid":
        res_ref, gate_ref = refs.pop(0), refs.pop(0)
    o_ref = refs.pop(0)

    if prologue:
        h_scr = refs.pop(0)
        rows = min(tm, 256)

        @pl.when(pl.program_id(1) == 0)
        def _():
            gmul = ng_ref[...] * (1.0 + sc_ref[0])
            shift = sh_ref[0]

            def body(rc, carry):
                r0 = pl.multiple_of(rc * rows, rows)
                xb = lhs_ref[pl.ds(r0, rows), :]
                ms = jnp.mean(xb * xb, axis=-1, keepdims=True)
                h = xb * lax.rsqrt(ms + EPS) * gmul + shift
                h_scr[pl.ds(r0, rows), :] = h.astype(BF)
                return carry

            lax.fori_loop(0, tm // rows, body, 0)

        lhs = h_scr[...]
    else:
        lhs = lhs_ref[...]

    acc = jnp.dot(lhs, w_ref[...], preferred_element_type=F32)
    if epilogue == "gnorm":
        ms = jnp.dot((acc * acc).astype(BF), gm_ref[0], preferred_element_type=F32)
        scale = jnp.where(flag_ref[...] > 0.0, lax.rsqrt(ms + EPS) * gain_ref[...], 1.0)
        out = acc * scale
    elif epilogue == "relu2":
        r = jnp.maximum(acc, 0.0)
        out = r * r
    elif epilogue == "resid":
        out = res_ref[...] + gate_ref[0] * acc
    else:
        out = acc
    o_ref[...] = out.astype(o_ref.dtype)


def _group_mats(gs, tn):
    n = gs.shape[0]
    col = np.arange(n)
    gsafe = np.maximum(gs, 1)
    grp = col // gsafe
    tile = col // tn
    r = np.arange(tn)
    out = np.zeros((n // tn, tn, tn), np.float32)
    for j in range(n // tn):
        cj = col[j * tn:(j + 1) * tn]
        same = (grp[cj][None, :] == ((r[:, None] + j * tn) // gsafe[cj][None, :]))
        val = np.where(gs[cj] > 0, 1.0 / gsafe[cj], 0.0)[None, :]
        out[j] = np.where(same, val, 0.0)
    del tile
    return jnp.asarray(out, dtype=BF)


def _matmul(lhs, w, *, seq, tm, tn, out_dtype, lhs_cols=None, prologue=None,
            gnorm=None, relu2=False, resid=None, name="mm"):
    t = lhs.shape[0]
    k, n = w.shape
    tm = min(tm, seq)
    cb = 0 if lhs_cols is None else lhs_cols[0]
    nb = seq // tm
    args, specs = [lhs], [pl.BlockSpec((tm, k), lambda i, j: (i, cb))]
    if prologue is not None:
        ng, sh, sc = prologue
        args += [ng.reshape(1, k), sh, sc]
        specs += [pl.BlockSpec((1, k), lambda i, j: (0, 0)),
                  pl.BlockSpec((1, 1, k), lambda i, j: (i // nb, 0, 0)),
                  pl.BlockSpec((1, 1, k), lambda i, j: (i // nb, 0, 0))]
    args.append(w)
    specs.append(pl.BlockSpec((k, tn), lambda i, j: (0, j)))
    epilogue = "none"
    if gnorm is not None:
        epilogue = "gnorm"
        gs, gain = gnorm
        flag = jnp.asarray((gs > 0).astype(np.float32)).reshape(1, n)
        args += [_group_mats(gs, tn), gain.reshape(1, n).astype(F32), flag]
        specs += [pl.BlockSpec((1, tn, tn), lambda i, j: (j, 0, 0)),
                  pl.BlockSpec((1, tn), lambda i, j: (0, j)),
                  pl.BlockSpec((1, tn), lambda i, j: (0, j))]
    if relu2:
        epilogue = "relu2"
    if resid is not None:
        epilogue = "resid"
        res, gate = resid
        args += [res, gate]
        specs += [pl.BlockSpec((tm, tn), lambda i, j: (i, j)),
                  pl.BlockSpec((1, 1, tn), lambda i, j: (i // nb, 0, j))]
    scratch = [pltpu.VMEM((tm, k), BF)] if prologue is not None else []
    return pl.pallas_call(
        functools.partial(_mm_kernel, prologue=prologue is not None, epilogue=epilogue, tm=tm),
        out_shape=jax.ShapeDtypeStruct((t, n), out_dtype),
        grid=(t // tm, n // tn),
        in_specs=specs,
        out_specs=pl.BlockSpec((tm, tn), lambda i, j: (i, j)),
        scratch_shapes=scratch,
        compiler_params=_cparams(("parallel", "arbitrary")),
        name=name,
    )(*args)


def _mmk_kernel(a_ref, w_ref, res_ref, gate_ref, o_ref, acc_ref):
    kk = pl.program_id(2)

    @pl.when(kk == 0)
    def _():
        acc_ref[...] = jnp.zeros_like(acc_ref)

    acc_ref[...] += jnp.dot(a_ref[...], w_ref[...], preferred_element_type=F32)

    @pl.when(kk == pl.num_programs(2) - 1)
    def _():
        o_ref[...] = res_ref[...] + gate_ref[0] * acc_ref[...]


def _matmul_ktiled_resid(a, w, res, gate, *, seq, tm, tn, tk, name):
    t, k = a.shape
    n = w.shape[1]
    tm = min(tm, seq)
    nb = seq // tm
    return pl.pallas_call(
        _mmk_kernel,
        out_shape=jax.ShapeDtypeStruct((t, n), F32),
        grid=(t // tm, n // tn, k // tk),
        in_specs=[pl.BlockSpec((tm, tk), lambda i, j, q: (i, q)),
                  pl.BlockSpec((tk, tn), lambda i, j, q: (q, j)),
                  pl.BlockSpec((tm, tn), lambda i, j, q: (i, j)),
                  pl.BlockSpec((1, 1, tn), lambda i, j, q: (i // nb, 0, j))],
        out_specs=pl.BlockSpec((tm, tn), lambda i, j, q: (i, j)),
        scratch_shapes=[pltpu.VMEM((tm, tn), F32)],
        compiler_params=_cparams(("parallel", "parallel", "arbitrary")),
        name=name,
    )(a, w, res, gate)


def _online_update(hd, s, vv, m_scr, l_scr, acc_scr):
    m_old = m_scr[hd]
    m_new = jnp.maximum(m_old, jnp.max(s, axis=-1, keepdims=True))
    alpha = jnp.exp(m_old - m_new)
    p = jnp.exp(s - m_new)
    l_scr[hd] = alpha * l_scr[hd] + jnp.sum(p, axis=-1, keepdims=True)
    acc_scr[hd] = alpha * acc_scr[hd] + jnp.dot(p.astype(BF), vv, preferred_element_type=F32)
    m_scr[hd] = m_new


def _init_state(m_scr, l_scr, acc_scr):
    m_scr[...] = jnp.full(m_scr.shape, NEG, F32)
    l_scr[...] = jnp.zeros(l_scr.shape, F32)
    acc_scr[...] = jnp.zeros(acc_scr.shape, F32)


def _swa_kernel(sink_ref, q_ref, kp_ref, kc_ref, vp_ref, vc_ref, o_ref, *, slopes, n_kv, group):
    i = pl.program_id(1)
    k = jnp.concatenate([kp_ref[...], kc_ref[...]], axis=0)
    v = jnp.concatenate([vp_ref[...], vc_ref[...]], axis=0)
    r = lax.broadcasted_iota(jnp.int32, (BLK, 2 * BLK), 0)
    c = lax.broadcasted_iota(jnp.int32, (BLK, 2 * BLK), 1)
    dist = r - c + BLK
    distf = dist.astype(F32)
    has_prev = jnp.where(i > 0, 1, 0)
    ok = jnp.where(dist >= 0, jnp.where(dist < BLK, jnp.where(c >= BLK, 1, has_prev), 0), 0)
    maskb = jnp.where(ok > 0, 0.0, NEG)
    for kv in range(n_kv):
        kk = k[:, kv * HEAD_DIM:(kv + 1) * HEAD_DIM]
        vv = v[:, kv * HEAD_DIM:(kv + 1) * HEAD_DIM]
        outs = []
        for g in range(group):
            h = kv * group + g
            qh = q_ref[:, h * HEAD_DIM:(h + 1) * HEAD_DIM]
            s = lax.dot_general(qh, kk, NT, preferred_element_type=F32) - slopes[h] * distf + maskb
            sink = sink_ref[h]
            m = jnp.maximum(jnp.max(s, axis=-1, keepdims=True), sink)
            p = jnp.exp(s - m)
            den = jnp.sum(p, axis=-1, keepdims=True) + jnp.exp(sink - m)
            o = jnp.dot(p.astype(BF), vv, preferred_element_type=F32) * (1.0 / den)
            outs.append(o)
        w = group * HEAD_DIM
        o_ref[:, kv * w:(kv + 1) * w] = jnp.concatenate(outs, axis=-1).astype(o_ref.dtype)


def _swa_attention(qkv, sinks, *, batch, seq, n_heads, n_kv):
    t = qkv.shape[0]
    nq = seq // BLK
    dq = n_heads * HEAD_DIM
    dkv = n_kv * HEAD_DIM
    kcol = dq // dkv
    cur = lambda col: (lambda b, i: (b * nq + i, col))
    prev = lambda col: (lambda b, i: (b * nq + jnp.maximum(i - 1, 0), col))
    return pl.pallas_call(
        functools.partial(_swa_kernel, slopes=_alibi_slopes(n_heads), n_kv=n_kv, group=n_heads // n_kv),
        out_shape=jax.ShapeDtypeStruct((t, dq), BF),
        grid=(batch, nq),
        in_specs=[pl.BlockSpec(memory_space=pltpu.SMEM),
                  pl.BlockSpec((BLK, dq), cur(0)),
                  pl.BlockSpec((BLK, dkv), prev(kcol)),
                  pl.BlockSpec((BLK, dkv), cur(kcol)),
                  pl.BlockSpec((BLK, dkv), prev(kcol + 1)),
                  pl.BlockSpec((BLK, dkv), cur(kcol + 1))],
        out_specs=pl.BlockSpec((BLK, dq), cur(0)),
        compiler_params=_cparams(("parallel", "parallel")),
        name="swa_attn",
    )(sinks.astype(F32), qkv, qkv, qkv, qkv, qkv)


def _dil_kernel(q10, q11, q20, q21, q30, q31, kv1, kv2, kv3, o_ref, m_scr, l_scr, acc_scr,
                *, slopes, seq, n_kv, group, nk1, nk2, ch3):
    i = pl.program_id(1)
    t0 = i * BLK
    _init_state(m_scr, l_scr, acc_scr)
    dkv = n_kv * HEAD_DIM
    per_ref = 512 // HEAD_DIM

    def chunk(qrefs, kv_ref, start, nk, win, dil):
        r = lax.broadcasted_iota(jnp.int32, (BLK, nk), 0)
        c = lax.broadcasted_iota(jnp.int32, (BLK, nk), 1)
        dist = (r - c) + (t0 - start)
        ok = jnp.where(dist >= 0, jnp.where(dist <= win, jnp.where((dist & (dil - 1)) == 0, 1, 0), 0), 0)
        bias = jnp.where(ok > 0, 0.0, NEG)
        distf = dist.astype(F32)
        for kv in range(n_kv):
            kk = kv_ref[pl.ds(start, nk), kv * HEAD_DIM:(kv + 1) * HEAD_DIM]
            vv = kv_ref[pl.ds(start, nk), dkv + kv * HEAD_DIM:dkv + (kv + 1) * HEAD_DIM]
            for g in range(group):
                hd = kv * group + g
                qref = qrefs[hd // per_ref]
                col = (hd % per_ref) * HEAD_DIM
                qh = qref[:, col:col + HEAD_DIM]
                s = lax.dot_general(qh, kk, NT, preferred_element_type=F32) - slopes[hd] * distf + bias
                _online_update(hd, s, vv, m_scr, l_scr, acc_scr)

    (w1, d1), (w2, d2), (w3, d3) = DIL_PATTERNS
    s1 = pl.multiple_of(jnp.clip(t0 - (nk1 - BLK), 0, seq - nk1), BLK)
    chunk((q10, q11), kv1, s1, nk1, w1, d1)
    s2 = pl.multiple_of(jnp.clip(t0 - (nk2 - BLK), 0, seq - nk2), BLK)
    chunk((q20, q21), kv2, s2, nk2, w2, d2)

    def body3(cc, carry):
        chunk((q30, q31), kv3, pl.multiple_of(cc * ch3, ch3), ch3, w3, d3)
        return carry

    lax.fori_loop(0, (t0 + BLK - 1) // ch3 + 1, body3, 0)

    n_heads = n_kv * group
    outs = [acc_scr[hd] * (1.0 / l_scr[hd]) for hd in range(n_heads)]
    o_ref[...] = jnp.concatenate(outs, axis=-1).astype(o_ref.dtype)


def _dil_attention(proj, *, batch, seq, n_heads, n_kv):
    t = proj.shape[0]
    nq = seq // BLK
    group_cols = (n_heads + 2 * n_kv) * HEAD_DIM
    assert group_cols % 512 == 0 and (n_heads * HEAD_DIM) == 1024
    gb = group_cols // 512
    qspec = lambda blk: pl.BlockSpec((BLK, 512), lambda b, i: (b * nq + i, blk))
    kvspec = lambda blk: pl.BlockSpec((seq, 512), lambda b, i: (b, blk))
    nk1 = min(2 * BLK, seq)
    nk2 = min(DIL_PATTERNS[1][0] + BLK, seq)
    ch3 = min(512, seq)
    in_specs = [qspec(g * gb + h) for g in range(3) for h in range(2)] + [kvspec(g * gb + 2) for g in range(3)]
    return pl.pallas_call(
        functools.partial(_dil_kernel, slopes=_alibi_slopes(n_heads), seq=seq, n_kv=n_kv,
                          group=n_heads // n_kv, nk1=nk1, nk2=nk2, ch3=ch3),
        out_shape=jax.ShapeDtypeStruct((t, n_heads * HEAD_DIM), BF),
        grid=(batch, nq),
        in_specs=in_specs,
        out_specs=pl.BlockSpec((BLK, n_heads * HEAD_DIM), lambda b, i: (b * nq + i, 0)),
        scratch_shapes=[pltpu.VMEM((n_heads, BLK, 1), F32),
                        pltpu.VMEM((n_heads, BLK, 1), F32),
                        pltpu.VMEM((n_heads, BLK, HEAD_DIM), F32)],
        compiler_params=_cparams(("parallel", "arbitrary")),
        name="dil_attn",
    )(*([proj] * 9))


def _dsa_kernel(q_ref, qi_ref, wq_ref, kv_ref, ki_ref, o_ref, sc_scr, m_scr, l_scr, acc_scr,
                *, slopes, topk, ch, n_heads, n_idx, hd_dim, idx_dim):
    i = pl.program_id(1)
    t0 = i * BLK
    nch = (t0 + BLK - 1) // ch + 1
    w = wq_ref[...].astype(F32)
    wscale = float(idx_dim ** -0.5) * float(n_idx ** -0.5)
    wcols = [w[:, idx_dim + h:idx_dim + h + 1] * wscale for h in range(n_idx)]
    r = lax.broadcasted_iota(jnp.int32, (BLK, ch), 0)
    c = lax.broadcasted_iota(jnp.int32, (BLK, ch), 1)
    rc = r - c

    def p1(cc, carry):
        start = pl.multiple_of(cc * ch, ch)
        kk = ki_ref[pl.ds(start, ch), 0:idx_dim]
        score = jnp.zeros((BLK, ch), F32)
        for h in range(n_idx):
            qh = qi_ref[:, h * idx_dim:(h + 1) * idx_dim]
            rel = lax.dot_general(qh, kk, NT, preferred_element_type=F32)
            score = score + wcols[h] * jnp.maximum(rel, 0.0)
        score = jnp.where(rc + (t0 - start) >= 0, score, -jnp.inf)
        sc_scr[:, pl.ds(start, ch)] = score
        return carry

    lax.fori_loop(0, nch, p1, 0)

    tpos = t0 + lax.broadcasted_iota(jnp.int32, (BLK, 1), 0)
    kq = jnp.minimum(tpos + 1, topk).astype(F32)

    def key_to_f32(key):
        bits = jnp.where(key < 0, key ^ jnp.int32(0x7FFFFFFF), key)
        return lax.bitcast_convert_type(bits, F32)

    def count_ge(cf):
        def body(cc, acc):
            start = pl.multiple_of(cc * ch, ch)
            return acc + jnp.where(sc_scr[:, pl.ds(start, ch)] >= cf, 1.0, 0.0)

        acc = lax.fori_loop(0, nch, body, jnp.zeros((BLK, ch), F32))
        return jnp.sum(acc, axis=-1, keepdims=True)

    ans0 = jnp.where(count_ge(jnp.zeros((BLK, 1), F32)) >= kq,
                     jnp.int32(0), jnp.int32(-2147483648))

    def radix(b, ans):
        cand = ans | lax.shift_left(jnp.int32(1), 30 - b)
        return jnp.where(count_ge(key_to_f32(cand)) >= kq, cand, ans)

    thr = key_to_f32(lax.fori_loop(0, 31, radix, ans0))

    def p3(cc, carry):
        start = pl.multiple_of(cc * ch, ch)
        sc_scr[:, pl.ds(start, ch)] = jnp.where(sc_scr[:, pl.ds(start, ch)] >= thr, 0.0, NEG)
        return carry

    lax.fori_loop(0, nch, p3, 0)

    _init_state(m_scr, l_scr, acc_scr)

    def p4(cc, carry):
        start = pl.multiple_of(cc * ch, ch)
        kk = kv_ref[pl.ds(start, ch), 0:hd_dim]
        vv = kv_ref[pl.ds(start, ch), hd_dim:2 * hd_dim]
        bias = sc_scr[:, pl.ds(start, ch)]
        distf = (rc + (t0 - start)).astype(F32)
        for h in range(n_heads):
            qh = q_ref[:, h * hd_dim:(h + 1) * hd_dim]
            s = lax.dot_general(qh, kk, NT, preferred_element_type=F32) - slopes[h] * distf + bias
            _online_update(h, s, vv, m_scr, l_scr, acc_scr)
        return carry

    lax.fori_loop(0, nch, p4, 0)
    outs = [acc_scr[h] * (1.0 / l_scr[h]) for h in range(n_heads)]
    o_ref[...] = jnp.concatenate(outs, axis=-1).astype(o_ref.dtype)


def _dsa_attention(qq, proj, kv, *, batch, seq, n_heads, hd_dim, n_idx, idx_dim, kidx_block):
    t = qq.shape[0]
    nq = seq // BLK
    dq = n_heads * hd_dim
    di = n_idx * idx_dim
    assert dq % di == 0
    topk = min(TOPK_MAX, seq // 4)
    ch = min(512, seq)
    return pl.pallas_call(
        functools.partial(_dsa_kernel, slopes=_alibi_slopes(n_heads), topk=topk, ch=ch,
                          n_heads=n_heads, n_idx=n_idx, hd_dim=hd_dim, idx_dim=idx_dim),
        out_shape=jax.ShapeDtypeStruct((t, dq), BF),
        grid=(batch, nq),
        in_specs=[pl.BlockSpec((BLK, dq), lambda b, i: (b * nq + i, 0)),
                  pl.BlockSpec((BLK, di), lambda b, i: (b * nq + i, dq // di)),
                  pl.BlockSpec((BLK, 128), lambda b, i: (b * nq + i, kidx_block)),
                  pl.BlockSpec((seq, 2 * hd_dim), lambda b, i: (b, 0)),
                  pl.BlockSpec((seq, 128), lambda b, i: (b, kidx_block))],
        out_specs=pl.BlockSpec((BLK, dq), lambda b, i: (b * nq + i, 0)),
        scratch_shapes=[pltpu.VMEM((BLK, seq), F32),
                        pltpu.VMEM((n_heads, BLK, 1), F32),
                        pltpu.VMEM((n_heads, BLK, 1), F32),
                        pltpu.VMEM((n_heads, BLK, hd_dim), F32)],
        compiler_params=_cparams(("parallel", "arbitrary")),
        name="dsa_attn",
    )(qq, qq, proj, kv, proj)


def _mixer_swa(xf, pro, gate, w_in, qn_g, kn_g, sinks, w_out, *, batch, seq):
    d = xf.shape[1]
    n_heads = d // HEAD_DIM
    n_kv = n_heads // 8
    dq, dkv = n_heads * HEAD_DIM, n_kv * HEAD_DIM
    gs = np.concatenate([np.full(dq + dkv, HEAD_DIM), np.zeros(dkv)]).astype(np.int64)
    gain = jnp.concatenate([jnp.tile(qn_g * (HEAD_DIM ** -0.5), n_heads), jnp.tile(kn_g, n_kv),
                            jnp.ones((dkv,), F32)])
    qkv = _matmul(xf, w_in.astype(BF), seq=seq, tm=1024, tn=512, out_dtype=BF, prologue=pro,
                  gnorm=(gs, gain), name="swa_in")
    o = _swa_attention(qkv, sinks, batch=batch, seq=seq, n_heads=n_heads, n_kv=n_kv)
    return _matmul(o, w_out.astype(BF), seq=seq, tm=1024, tn=512, out_dtype=F32,
                   resid=(xf, gate), name="swa_out")


def _mixer_dsa(xf, pro, gate, w_in, qlat_g, kvlat_g, w_uq, w_ukv, w_idx_q, qn_g, kn_g, w_out,
               *, batch, seq):
    d = xf.shape[1]
    qrank, kvrank = qlat_g.shape[0], kvlat_g.shape[0]
    hd_dim = kn_g.shape[0]
    n_heads = w_uq.shape[1] // hd_dim
    n_in = w_in.shape[1]
    idx_dim = n_in - qrank - kvrank - 16
    n_idx = 16
    assert idx_dim + n_idx <= 128 and (qrank + kvrank) % 128 == 0
    pad = qrank + kvrank + 128 - n_in
    w_in_p = jnp.pad(w_in, ((0, 0), (0, pad))).astype(BF)
    gs = np.concatenate([np.full(qrank, qrank), np.full(kvrank, kvrank), np.zeros(128)]).astype(np.int64)
    gain = jnp.concatenate([qlat_g, kvlat_g, jnp.ones((128,), F32)])
    proj = _matmul(xf, w_in_p, seq=seq, tm=512, tn=qrank + kvrank + 128, out_dtype=BF, prologue=pro,
                   gnorm=(gs, gain), name="dsa_in")
    w_q = jnp.concatenate([w_uq, w_idx_q], axis=1).astype(BF)
    nq_cols, ni_cols = w_uq.shape[1], w_idx_q.shape[1]
    gs_q = np.concatenate([np.full(nq_cols, hd_dim), np.zeros(ni_cols)]).astype(np.int64)
    gain_q = jnp.concatenate([jnp.tile(qn_g * (hd_dim ** -0.5), n_heads), jnp.ones((ni_cols,), F32)])
    qq = _matmul(proj, w_q, seq=seq, tm=1024, tn=512, out_dtype=BF, lhs_cols=(0, qrank),
                 gnorm=(gs_q, gain_q), name="dsa_q")
    gs_kv = np.concatenate([np.full(hd_dim, hd_dim), np.zeros(hd_dim)]).astype(np.int64)
    gain_kv = jnp.concatenate([kn_g, jnp.ones((hd_dim,), F32)])
    kv = _matmul(proj, w_ukv.astype(BF), seq=seq, tm=1024, tn=2 * hd_dim, out_dtype=BF,
                 lhs_cols=(qrank // kvrank, kvrank), gnorm=(gs_kv, gain_kv), name="dsa_kv")
    o = _dsa_attention(qq, proj, kv, batch=batch, seq=seq, n_heads=n_heads, hd_dim=hd_dim,
                       n_idx=n_idx, idx_dim=idx_dim, kidx_block=(qrank + kvrank) // 128)
    return _matmul(o, w_out.astype(BF), seq=seq, tm=1024, tn=512, out_dtype=F32,
                   resid=(xf, gate), name="dsa_out")


def _mixer_dil(xf, pro, gate, w_in, qn_g, kn_g, w_out, *, batch, seq):
    n_groups = len(DIL_PATTERNS)
    n_heads = w_out.shape[0] // HEAD_DIM
    n_kv = n_heads // 4
    dq, dkv = n_heads * HEAD_DIM, n_kv * HEAD_DIM
    gs = np.tile(np.concatenate([np.full(dq + dkv, HEAD_DIM), np.zeros(dkv)]), n_groups).astype(np.int64)
    gain = jnp.concatenate([
        jnp.concatenate([jnp.tile(qn_g[g] * (HEAD_DIM ** -0.5), n_heads), jnp.tile(kn_g[g], n_kv),
                         jnp.ones((dkv,), F32)]) for g in range(n_groups)])
    proj = _matmul(xf, w_in.astype(BF), seq=seq, tm=1024, tn=512, out_dtype=BF, prologue=pro,
                   gnorm=(gs, gain), name="dil_in")
    o = _dil_attention(proj, batch=batch, seq=seq, n_heads=n_heads, n_kv=n_kv)
    return _matmul(o, w_out.astype(BF), seq=seq, tm=1024, tn=512, out_dtype=F32,
                   resid=(xf, gate), name="dil_out")


def _mlp(xf, pro, gate, w1, w2, *, seq):
    u = _matmul(xf, w1.astype(BF), seq=seq, tm=1024, tn=512, out_dtype=BF, prologue=pro,
                relu2=True, name="mlp_up")
    return _matmul_ktiled_resid(u, w2.astype(BF), xf, gate, seq=seq, tm=1024, tn=1024, tk=1024,
                                name="mlp_down")


def kernel(x, c, ada_w, ada_b, norm1_g, norm2_g, mlp_w1, mlp_w2, swa_w_in, swa_qn_g, swa_kn_g, swa_sinks, swa_w_out, dsa_w_in, dsa_qlat_g, dsa_kvlat_g, dsa_w_uq, dsa_w_ukv, dsa_w_idx_q, dsa_qn_g, dsa_kn_g, dsa_w_out, dil_w_in, dil_qn_g, dil_kn_g, dil_w_out):
    batch, seq, d = x.shape
    depth = ada_w.shape[0]
    mod = _adaln_mod(c, ada_w, ada_b)
    xf = x.reshape(batch * seq, d)
    for i in range(depth):
        m = mod[i].reshape(batch, 1, 6 * d)
        sh1, sc1, g1, sh2, sc2, g2 = [m[:, :, q * d:(q + 1) * d] for q in range(6)]
        pro = (norm1_g[i], sh1, sc1)
        kind, j = i % 3, i // 3
        if kind == 0:
            xf = _mixer_swa(xf, pro, g1, swa_w_in[j], swa_qn_g[j], swa_kn_g[j], swa_sinks[j],
                            swa_w_out[j], batch=batch, seq=seq)
        elif kind == 1:
            xf = _mixer_dsa(xf, pro, g1, dsa_w_in[j], dsa_qlat_g[j], dsa_kvlat_g[j], dsa_w_uq[j],
                            dsa_w_ukv[j], dsa_w_idx_q[j], dsa_qn_g[j], dsa_kn_g[j], dsa_w_out[j],
                            batch=batch, seq=seq)
        else:
            xf = _mixer_dil(xf, pro, g1, dil_w_in[j], dil_qn_g[j], dil_kn_g[j], dil_w_out[j],
                            batch=batch, seq=seq)
        xf = _mlp(xf, (norm2_g[i], sh2, sc2), g2, mlp_w1[i], mlp_w2[i], seq=seq)
    return xf.reshape(batch, seq, d)
```

```python
import functools

import jax
import jax.numpy as jnp
import numpy as np
from jax import lax
from jax.experimental import pallas as pl
from jax.experimental.pallas import tpu as pltpu

F32 = jnp.float32
BF = jnp.bfloat16
NT = (((1,), (1,)), ((), ()))

EPS = 1e-6
NEG = -1e30
BIG = 1e30
LOG2E = 1.4426950408889634
BLK = 128
HEAD_DIM = 64
TOPK_MAX = 256
DIL_PATTERNS = ((128, 1), (512, 4), (2048, 16))
VMEM_LIMIT = 56 * 1024 * 1024


def _alibi_slopes(n):
    return [float(2.0 ** (-8.0 * (i + 1) / n)) * LOG2E for i in range(n)]


def _cparams(sem):
    return pltpu.CompilerParams(dimension_semantics=sem, vmem_limit_bytes=VMEM_LIMIT)


def _mod_kernel(c_ref, w_ref, b_ref, o_ref):
    c = c_ref[...]
    cond = c * (1.0 / (1.0 + jnp.exp(-c)))
    o_ref[0] = jnp.dot(cond.astype(BF), w_ref[0].astype(BF),
                       preferred_element_type=F32) + b_ref[0]


def _adaln_mod(c, ada_w, ada_b):
    depth, d, n = ada_w.shape
    b = c.shape[0]
    tn = 1024
    return pl.pallas_call(
        _mod_kernel,
        out_shape=jax.ShapeDtypeStruct((depth, b, n), F32),
        grid=(depth, n // tn),
        in_specs=[pl.BlockSpec((b, d), lambda l, j: (0, 0)),
                  pl.BlockSpec((1, d, tn), lambda l, j: (l, 0, j)),
                  pl.BlockSpec((1, 1, tn), lambda l, j: (l, 0, j))],
        out_specs=pl.BlockSpec((1, b, tn), lambda l, j: (l, 0, j)),
        compiler_params=_cparams(("parallel", "parallel")),
        name="adaln_mod",
    )(c, ada_w, ada_b.reshape(depth, 1, n))


def _mm_kernel(*refs, prologue, epilogue, tm):
    refs = list(refs)
    lhs_ref = refs.pop(0)
    if prologue:
        ng_ref, sh_ref, sc_ref = refs.pop(0), refs.pop(0), refs.pop(0)
    w_ref = refs.pop(0)
    if epilogue == "gnorm":
        gm_ref, gain_ref, flag_ref = refs.pop(0), refs.pop(0), refs.pop(0)
    if epilogue == "resid":
        res_ref, gate_ref = refs.pop(0), refs.pop(0)
    o_ref = refs.pop(0)

    if prologue:
        h_scr = refs.pop(0)
        rows = min(tm, 256)

        @pl.when(pl.program_id(1) == 0)
        def _():
            gmul = ng_ref[...] * (1.0 + sc_ref[0])
            shift = sh_ref[0]

            def body(rc, carry):
                r0 = pl.multiple_of(rc * rows, rows)
                xb = lhs_ref[pl.ds(r0, rows), :]
                ms = jnp.mean(xb * xb, axis=-1, keepdims=True)
                h = xb * lax.rsqrt(ms + EPS) * gmul + shift
                h_scr[pl.ds(r0, rows), :] = h.astype(BF)
                return carry

            lax.fori_loop(0, tm // rows, body, 0)

        lhs = h_scr[...]
    else:
        lhs = lhs_ref[...]

    acc = jnp.dot(lhs, w_ref[...], preferred_element_type=F32)
    if epilogue == "gnorm":
        ms = jnp.dot((acc * acc).astype(BF), gm_ref[0], preferred_element_type=F32)
        scale = jnp.where(flag_ref[...] > 0.0, lax.rsqrt(ms + EPS) * gain_ref[...], 1.0)
        out = acc * scale
    elif epilogue == "relu2":
        r = jnp.maximum(acc, 0.0)
        out = r * r
    elif epilogue == "resid":
        out = res_ref[...] + gate_ref[0] * acc
    else:
        out = acc
    o_ref[...] = out.astype(o_ref.dtype)


def _group_mats(gs, tn):
    n = gs.shape[0]
    col = np.arange(n)
    gsafe = np.maximum(gs, 1)
    grp = col // gsafe
    r = np.arange(tn)
    out = np.zeros((n // tn, tn, tn), np.float32)
    for j in range(n // tn):
        cj = col[j * tn:(j + 1) * tn]
        same = (grp[cj][None, :] == ((r[:, None] + j * tn) // gsafe[cj][None, :]))
        val = np.where(gs[cj] > 0, 1.0 / gsafe[cj], 0.0)[None, :]
        out[j] = np.where(same, val, 0.0)
    return jnp.asarray(out, dtype=BF)


def _matmul(lhs, w, *, seq, tm, tn, out_dtype, lhs_cols=None, prologue=None,
            gnorm=None, relu2=False, resid=None, name="mm"):
    t = lhs.shape[0]
    k, n = w.shape
    tm = min(tm, seq)
    cb = 0 if lhs_cols is None else lhs_cols[0]
    nb = seq // tm
    args, specs = [lhs], [pl.BlockSpec((tm, k), lambda i, j: (i, cb))]
    if prologue is not None:
        ng, sh, sc = prologue
        args += [ng.reshape(1, k), sh, sc]
        specs += [pl.BlockSpec((1, k), lambda i, j: (0, 0)),
                  pl.BlockSpec((1, 1, k), lambda i, j: (i // nb, 0, 0)),
                  pl.BlockSpec((1, 1, k), lambda i, j: (i // nb, 0, 0))]
    args.append(w)
    specs.append(pl.BlockSpec((k, tn), lambda i, j: (0, j)))
    epilogue = "none"
    if gnorm is not None:
        epilogue = "gnorm"
        gs, gain = gnorm
        flag = jnp.asarray((gs > 0).astype(np.float32)).reshape(1, n)
        args += [_group_mats(gs, tn), gain.reshape(1, n).astype(F32), flag]
        specs += [pl.BlockSpec((1, tn, tn), lambda i, j: (j, 0, 0)),
                  pl.BlockSpec((1, tn), lambda i, j: (0, j)),
                  pl.BlockSpec((1, tn), lambda i, j: (0, j))]
    if relu2:
        epilogue = "relu2"
    if resid is not None:
        epilogue = "resid"
        res, gate = resid
        args += [res, gate]
        specs += [pl.BlockSpec((tm, tn), lambda i, j: (i, j)),
                  pl.BlockSpec((1, 1, tn), lambda i, j: (i // nb, 0, j))]
    scratch = [pltpu.VMEM((tm, k), BF)] if prologue is not None else []
    return pl.pallas_call(
        functools.partial(_mm_kernel, prologue=prologue is not None, epilogue=epilogue, tm=tm),
        out_shape=jax.ShapeDtypeStruct((t, n), out_dtype),
        grid=(t // tm, n // tn),
        in_specs=specs,
        out_specs=pl.BlockSpec((tm, tn), lambda i, j: (i, j)),
        scratch_shapes=scratch,
        compiler_params=_cparams(("parallel", "arbitrary")),
        name=name,
    )(*args)


def _mmk_kernel(a_ref, w_ref, res_ref, gate_ref, o_ref, acc_ref):
    kk = pl.program_id(2)

    @pl.when(kk == 0)
    def _():
        acc_ref[...] = jnp.zeros_like(acc_ref)

    acc_ref[...] += jnp.dot(a_ref[...], w_ref[...], preferred_element_type=F32)

    @pl.when(kk == pl.num_programs(2) - 1)
    def _():
        o_ref[...] = res_ref[...] + gate_ref[0] * acc_ref[...]


def _matmul_ktiled_resid(a, w, res, gate, *, seq, tm, tn, tk, name):
    t, k = a.shape
    n = w.shape[1]
    tm = min(tm, seq)
    nb = seq // tm
    return pl.pallas_call(
        _mmk_kernel,
        out_shape=jax.ShapeDtypeStruct((t, n), F32),
        grid=(t // tm, n // tn, k // tk),
        in_specs=[pl.BlockSpec((tm, tk), lambda i, j, q: (i, q)),
                  pl.BlockSpec((tk, tn), lambda i, j, q: (q, j)),
                  pl.BlockSpec((tm, tn), lambda i, j, q: (i, j)),
                  pl.BlockSpec((1, 1, tn), lambda i, j, q: (i // nb, 0, j))],
        out_specs=pl.BlockSpec((tm, tn), lambda i, j, q: (i, j)),
        scratch_shapes=[pltpu.VMEM((tm, tn), F32)],
        compiler_params=_cparams(("parallel", "parallel", "arbitrary")),
        name=name,
    )(a, w, res, gate)


def _split_heads_k(tile, odd):
    nk = tile.shape[0]
    z = jnp.zeros((nk, HEAD_DIM), tile.dtype)
    half = tile[:, HEAD_DIM:] if odd else tile[:, :HEAD_DIM]
    return jnp.concatenate([half, z], axis=1), jnp.concatenate([z, half], axis=1)


def _masked_dist(dist, ok):
    return jnp.where(ok > 0, dist.astype(F32), BIG)


def _swa_kernel(sink_ref, q_ref, kp_ref, kc_ref, vp_ref, vc_ref, o_ref, *, slopes, n_kv, group):
    i = pl.program_id(1)
    k = jnp.concatenate([kp_ref[...], kc_ref[...]], axis=0)
    v = jnp.concatenate([vp_ref[...], vc_ref[...]], axis=0)
    vt = v.T
    nk = 2 * BLK
    row = lax.broadcasted_iota(jnp.int32, (nk, BLK), 0)
    col = lax.broadcasted_iota(jnp.int32, (nk, BLK), 1)
    dist = col - row + BLK
    has_prev = jnp.where(i > 0, 1, 0)
    ok = jnp.where(dist >= 0, jnp.where(dist < BLK, jnp.where(row >= BLK, 1, has_prev), 0), 0)
    dm = _masked_dist(dist, ok)
    pairs = group // 2
    for kv in range(n_kv):
        ka, kb = _split_heads_k(k[:, (kv // 2) * BLK:(kv // 2 + 1) * BLK], kv % 2 == 1)
        qst = jnp.concatenate([q_ref[:, (kv * pairs + p) * BLK:(kv * pairs + p + 1) * BLK]
                               for p in range(pairs)], axis=0)
        s_even = lax.dot_general(ka, qst, NT, preferred_element_type=F32)
        s_odd = lax.dot_general(kb, qst, NT, preferred_element_type=F32)
        pts, invs = [], []
        for g in range(group):
            p_, odd = divmod(g, 2)
            h = kv * group + g
            s = (s_odd if odd else s_even)[:, p_ * BLK:(p_ + 1) * BLK] - slopes[h] * dm
            sink = sink_ref[h] * LOG2E
            m = jnp.maximum(jnp.max(s, axis=0, keepdims=True), sink)
            p = jnp.exp2(s - m)
            den = jnp.sum(p, axis=0, keepdims=True) + jnp.exp2(sink - m)
            pts.append(p.astype(BF))
            invs.append(1.0 / den)
        acc = jnp.dot(vt[kv * HEAD_DIM:(kv + 1) * HEAD_DIM, :], jnp.concatenate(pts, axis=1),
                      preferred_element_type=F32)
        for p_ in range(pairs):
            oe = acc[:, (2 * p_) * BLK:(2 * p_ + 1) * BLK] * invs[2 * p_]
            oo = acc[:, (2 * p_ + 1) * BLK:(2 * p_ + 2) * BLK] * invs[2 * p_ + 1]
            c0 = (kv * pairs + p_) * BLK
            o_ref[:, c0:c0 + BLK] = jnp.concatenate([oe, oo], axis=0).T.astype(o_ref.dtype)


def _swa_attention(qkv, sinks, *, batch, seq, n_heads, n_kv):
    t = qkv.shape[0]
    nq = seq // BLK
    dq = n_heads * HEAD_DIM
    dkv = n_kv * HEAD_DIM
    kcol = dq // dkv
    cur = lambda col: (lambda b, i: (b * nq + i, col))
    prev = lambda col: (lambda b, i: (b * nq + jnp.maximum(i - 1, 0), col))
    return pl.pallas_call(
        functools.partial(_swa_kernel, slopes=_alibi_slopes(n_heads), n_kv=n_kv, group=n_heads // n_kv),
        out_shape=jax.ShapeDtypeStruct((t, dq), BF),
        grid=(batch, nq),
        in_specs=[pl.BlockSpec(memory_space=pltpu.SMEM),
                  pl.BlockSpec((BLK, dq), cur(0)),
                  pl.BlockSpec((BLK, dkv), prev(kcol)),
                  pl.BlockSpec((BLK, dkv), cur(kcol)),
                  pl.BlockSpec((BLK, dkv), prev(kcol + 1)),
                  pl.BlockSpec((BLK, dkv), cur(kcol + 1))],
        out_specs=pl.BlockSpec((BLK, dq), cur(0)),
        compiler_params=_cparams(("parallel", "parallel")),
        name="swa_attn",
    )(sinks.astype(F32), qkv, qkv, qkv, qkv, qkv)


def _dil_kernel(q10, q11, q20, q21, q30, q31, kv1, kv2, kv3, o_ref, vt_scr, m_scr, l_scr, acc_scr,
                *, slopes, seq, n_kv, group, nk1, nk2, ch3):
    i = pl.program_id(1)
    t0 = i * BLK
    dkv = n_kv * HEAD_DIM
    n_heads = n_kv * group
    pairs = group // 2
    tiles_per_ref = 512 // BLK

    @pl.when(i == 0)
    def _():
        for g, kvr in enumerate((kv1, kv2, kv3)):
            vt_scr[g] = kvr[:, dkv:2 * dkv].T

    m_scr[...] = jnp.full(m_scr.shape, NEG, F32)
    l_scr[...] = jnp.zeros(l_scr.shape, F32)
    acc_scr[...] = jnp.zeros(acc_scr.shape, F32)

    def chunk(g, qrefs, kv_ref, start, nk, win, dil):
        row = lax.broadcasted_iota(jnp.int32, (nk, BLK), 0)
        col = lax.broadcasted_iota(jnp.int32, (nk, BLK), 1)
        dist = (col - row) + (t0 - start)
        ok = jnp.where(dist >= 0, jnp.where(dist <= win, jnp.where((dist & (dil - 1)) == 0, 1, 0), 0), 0)
        dm = _masked_dist(dist, ok)
        for kv in range(n_kv):
            c0 = (kv // 2) * BLK
            ka, kb = _split_heads_k(kv_ref[pl.ds(start, nk), c0:c0 + BLK], kv % 2 == 1)
            tiles = []
            for p_ in range(pairs):
                tq = kv * pairs + p_
                cq = (tq % tiles_per_ref) * BLK
                tiles.append(qrefs[tq // tiles_per_ref][:, cq:cq + BLK])
            qst = jnp.concatenate(tiles, axis=0)
            s_even = lax.dot_general(ka, qst, NT, preferred_element_type=F32)
            s_odd = lax.dot_general(kb, qst, NT, preferred_element_type=F32)
            pts, alphas = [], []
            for g_ in range(group):
                p_, odd = divmod(g_, 2)
                hd = kv * group + g_
                s = (s_odd if odd else s_even)[:, p_ * BLK:(p_ + 1) * BLK] - slopes[hd] * dm
                m_old = m_scr[hd]
                m_new = jnp.maximum(m_old, jnp.max(s, axis=0, keepdims=True))
                alpha = jnp.exp2(m_old - m_new)
                p = jnp.exp2(s - m_new)
                l_scr[hd] = alpha * l_scr[hd] + jnp.sum(p, axis=0, keepdims=True)
                m_scr[hd] = m_new
                pts.append(p.astype(BF))
                alphas.append(alpha)
            pv = jnp.dot(vt_scr[g, kv * HEAD_DIM:(kv + 1) * HEAD_DIM, pl.ds(start, nk)],
                         jnp.concatenate(pts, axis=1), preferred_element_type=F32)
            for g_ in range(group):
                hd = kv * group + g_
                acc_scr[hd] = alphas[g_] * acc_scr[hd] + pv[:, g_ * BLK:(g_ + 1) * BLK]

    (w1, d1), (w2, d2), (w3, d3) = DIL_PATTERNS
    s1 = pl.multiple_of(jnp.clip(t0 - (nk1 - BLK), 0, seq - nk1), BLK)
    chunk(0, (q10, q11), kv1, s1, nk1, w1, d1)
    s2 = pl.multiple_of(jnp.clip(t0 - (nk2 - BLK), 0, seq - nk2), BLK)
    chunk(1, (q20, q21), kv2, s2, nk2, w2, d2)

    def body3(cc, carry):
        chunk(2, (q30, q31), kv3, pl.multiple_of(cc * ch3, ch3), ch3, w3, d3)
        return carry

    lax.fori_loop(0, (t0 + BLK - 1) // ch3 + 1, body3, 0)

    for tq in range(n_heads // 2):
        oe = acc_scr[2 * tq] * (1.0 / l_scr[2 * tq])
        oo = acc_scr[2 * tq + 1] * (1.0 / l_scr[2 * tq + 1])
        o_ref[:, tq * BLK:(tq + 1) * BLK] = jnp.concatenate([oe, oo], axis=0).T.astype(o_ref.dtype)


def _dil_attention(proj, *, batch, seq, n_heads, n_kv):
    t = proj.shape[0]
    nq = seq // BLK
    group_cols = (n_heads + 2 * n_kv) * HEAD_DIM
    assert group_cols % 512 == 0 and (n_heads * HEAD_DIM) == 1024 and 2 * n_kv * HEAD_DIM == 512
    gb = group_cols // 512
    qspec = lambda blk: pl.BlockSpec((BLK, 512), lambda b, i: (b * nq + i, blk))
    kvspec = lambda blk: pl.BlockSpec((seq, 512), lambda b, i: (b, blk))
    nk1 = min(2 * BLK, seq)
    nk2 = min(DIL_PATTERNS[1][0] + BLK, seq)
    ch3 = min(512, seq)
    in_specs = [qspec(g * gb + h) for g in range(3) for h in range(2)] + [kvspec(g * gb + 2) for g in range(3)]
    return pl.pallas_call(
        functools.partial(_dil_kernel, slopes=_alibi_slopes(n_heads), seq=seq, n_kv=n_kv,
                          group=n_heads // n_kv, nk1=nk1, nk2=nk2, ch3=ch3),
        out_shape=jax.ShapeDtypeStruct((t, n_heads * HEAD_DIM), BF),
        grid=(batch, nq),
        in_specs=in_specs,
        out_specs=pl.BlockSpec((BLK, n_heads * HEAD_DIM), lambda b, i: (b * nq + i, 0)),
        scratch_shapes=[pltpu.VMEM((3, n_kv * HEAD_DIM, seq), BF),
                        pltpu.VMEM((n_heads, 1, BLK), F32),
                        pltpu.VMEM((n_heads, 1, BLK), F32),
                        pltpu.VMEM((n_heads, HEAD_DIM, BLK), F32)],
        compiler_params=_cparams(("parallel", "arbitrary")),
        name="dil_attn",
    )(*([proj] * 9))


def _dsa_kernel(q_ref, qi_ref, wq_ref, kv_ref, ki_ref, o_ref,
                vt_scr, ka_scr, kb_scr, qst_scr, qist_scr, sc_scr, m_scr, l_scr, acc_scr,
                *, slopes, topk, ch, n_heads, n_idx, hd_dim, idx_dim):
    i = pl.program_id(1)
    t0 = i * BLK
    nch = (t0 + BLK - 1) // ch + 1

    @pl.when(i == 0)
    def _():
        vt_scr[...] = kv_ref[:, hd_dim:2 * hd_dim].T
        ka, kb = _split_heads_k(ki_ref[...], False)
        ka_scr[...] = ka
        kb_scr[...] = kb

    for h in range(n_heads):
        qst_scr[h * BLK:(h + 1) * BLK, :] = q_ref[:, h * hd_dim:(h + 1) * hd_dim]
    for p in range(n_idx // 2):
        qist_scr[p * BLK:(p + 1) * BLK, :] = qi_ref[:, p * BLK:(p + 1) * BLK]

    wscale = float(idx_dim ** -0.5) * float(n_idx ** -0.5)
    wt = wq_ref[...].astype(F32).T * wscale
    row = lax.broadcasted_iota(jnp.int32, (ch, BLK), 0)
    col = lax.broadcasted_iota(jnp.int32, (ch, BLK), 1)
    cr = col - row

    def p1(cc, carry):
        start = pl.multiple_of(cc * ch, ch)
        qi = qist_scr[...]
        rel_e = lax.dot_general(ka_scr[pl.ds(start, ch), :], qi, NT, preferred_element_type=F32)
        rel_o = lax.dot_general(kb_scr[pl.ds(start, ch), :], qi, NT, preferred_element_type=F32)
        score = jnp.zeros((ch, BLK), F32)
        for p in range(n_idx // 2):
            we = wt[idx_dim + 2 * p:idx_dim + 2 * p + 1, :]
            wo = wt[idx_dim + 2 * p + 1:idx_dim + 2 * p + 2, :]
            score = score + we * jnp.maximum(rel_e[:, p * BLK:(p + 1) * BLK], 0.0)
            score = score + wo * jnp.maximum(rel_o[:, p * BLK:(p + 1) * BLK], 0.0)
        sc_scr[pl.ds(start, ch), :] = jnp.where(cr + (t0 - start) >= 0, score, -jnp.inf)
        return carry

    lax.fori_loop(0, nch, p1, 0)

    tq = t0 + lax.broadcasted_iota(jnp.int32, (1, BLK), 1)
    kq = jnp.minimum(tq + 1, topk).astype(F32)

    def key_to_f32(key):
        bits = jnp.where(key < 0, key ^ jnp.int32(0x7FFFFFFF), key)
        return lax.bitcast_convert_type(bits, F32)

    def count_ge(cf):
        lanes = 64

        def body(cc, acc):
            blk = sc_scr[pl.ds(pl.multiple_of(cc * ch, ch), ch), :]
            hit = jnp.where(blk >= cf, 1.0, 0.0)
            for j in range(ch // lanes):
                acc = acc + hit[j * lanes:(j + 1) * lanes, :]
            return acc

        acc = lax.fori_loop(0, nch, body, jnp.zeros((lanes, BLK), F32))
        return jnp.sum(acc, axis=0, keepdims=True)

    ans0 = jnp.where(count_ge(jnp.zeros((1, BLK), F32)) >= kq,
                     jnp.int32(0), jnp.int32(-2147483648))

    def radix(b, ans):
        cand = ans | lax.shift_left(jnp.int32(1), 30 - b)
        return jnp.where(count_ge(key_to_f32(cand)) >= kq, cand, ans)

    thr = key_to_f32(lax.fori_loop(0, 31, radix, ans0))

    def p3(cc, carry):
        start = pl.multiple_of(cc * ch, ch)
        sel = sc_scr[pl.ds(start, ch), :] >= thr
        sc_scr[pl.ds(start, ch), :] = jnp.where(sel, (cr + (t0 - start)).astype(F32), BIG)
        return carry

    lax.fori_loop(0, nch, p3, 0)

    m_scr[...] = jnp.full(m_scr.shape, NEG, F32)
    l_scr[...] = jnp.zeros(l_scr.shape, F32)
    acc_scr[...] = jnp.zeros(acc_scr.shape, F32)

    def p4(cc, carry):
        start = pl.multiple_of(cc * ch, ch)
        s_all = lax.dot_general(kv_ref[pl.ds(start, ch), 0:hd_dim], qst_scr[...], NT,
                                preferred_element_type=F32)
        dm = sc_scr[pl.ds(start, ch), :]
        pts, alphas = [], []
        for h in range(n_heads):
            hs = slice(h * BLK, (h + 1) * BLK)
            s = s_all[:, hs] - slopes[h] * dm
            m_old = m_scr[:, hs]
            m_new = jnp.maximum(m_old, jnp.max(s, axis=0, keepdims=True))
            alpha = jnp.exp2(m_old - m_new)
            p = jnp.exp2(s - m_new)
            l_scr[:, hs] = alpha * l_scr[:, hs] + jnp.sum(p, axis=0, keepdims=True)
            m_scr[:, hs] = m_new
            pts.append(p.astype(BF))
            alphas.append(alpha)
        pv = jnp.dot(vt_scr[:, pl.ds(start, ch)], jnp.concatenate(pts, axis=1),
                     preferred_element_type=F32)
        acc_scr[...] = jnp.concatenate(alphas, axis=1) * acc_scr[...] + pv
        return carry

    lax.fori_loop(0, nch, p4, 0)
    for h in range(n_heads):
        hs = slice(h * BLK, (h + 1) * BLK)
        o_ref[:, h * hd_dim:(h + 1) * hd_dim] = (acc_scr[:, hs] * (1.0 / l_scr[:, hs])).T.astype(o_ref.dtype)


def _dsa_attention(qq, proj, kv, *, batch, seq, n_heads, hd_dim, n_idx, idx_dim, kidx_block):
    t = qq.shape[0]
    nq = seq // BLK
    dq = n_heads * hd_dim
    di = n_idx * idx_dim
    assert dq % di == 0 and hd_dim == BLK and idx_dim == HEAD_DIM
    topk = min(TOPK_MAX, seq // 4)
    ch = min(512, seq)
    return pl.pallas_call(
        functools.partial(_dsa_kernel, slopes=_alibi_slopes(n_heads), topk=topk, ch=ch,
                          n_heads=n_heads, n_idx=n_idx, hd_dim=hd_dim, idx_dim=idx_dim),
        out_shape=jax.ShapeDtypeStruct((t, dq), BF),
        grid=(batch, nq),
        in_specs=[pl.BlockSpec((BLK, dq), lambda b, i: (b * nq + i, 0)),
                  pl.BlockSpec((BLK, di), lambda b, i: (b * nq + i, dq // di)),
                  pl.BlockSpec((BLK, BLK), lambda b, i: (b * nq + i, kidx_block)),
                  pl.BlockSpec((seq, 2 * hd_dim), lambda b, i: (b, 0)),
                  pl.BlockSpec((seq, BLK), lambda b, i: (b, kidx_block))],
        out_specs=pl.BlockSpec((BLK, dq), lambda b, i: (b * nq + i, 0)),
        scratch_shapes=[pltpu.VMEM((hd_dim, seq), BF),
                        pltpu.VMEM((seq, BLK), BF),
                        pltpu.VMEM((seq, BLK), BF),
                        pltpu.VMEM((n_heads * BLK, hd_dim), BF),
                        pltpu.VMEM((n_idx // 2 * BLK, BLK), BF),
                        pltpu.VMEM((seq, BLK), F32),
                        pltpu.VMEM((1, n_heads * BLK), F32),
                        pltpu.VMEM((1, n_heads * BLK), F32),
                        pltpu.VMEM((hd_dim, n_heads * BLK), F32)],
        compiler_params=_cparams(("parallel", "arbitrary")),
        name="dsa_attn",
    )(qq, qq, proj, kv, proj)


def _mixer_swa(xf, pro, gate, w_in, qn_g, kn_g, sinks, w_out, *, batch, seq):
    d = xf.shape[1]
    n_heads = d // HEAD_DIM
    n_kv = n_heads // 8
    dq, dkv = n_heads * HEAD_DIM, n_kv * HEAD_DIM
    gs = np.concatenate([np.full(dq + dkv, HEAD_DIM), np.zeros(dkv)]).astype(np.int64)
    gain = jnp.concatenate([jnp.tile(qn_g * (HEAD_DIM ** -0.5 * LOG2E), n_heads), jnp.tile(kn_g, n_kv),
                            jnp.ones((dkv,), F32)])
    qkv = _matmul(xf, w_in.astype(BF), seq=seq, tm=1024, tn=512, out_dtype=BF, prologue=pro,
                  gnorm=(gs, gain), name="swa_in")
    o = _swa_attention(qkv, sinks, batch=batch, seq=seq, n_heads=n_heads, n_kv=n_kv)
    return _matmul(o, w_out.astype(BF), seq=seq, tm=1024, tn=512, out_dtype=F32,
                   resid=(xf, gate), name="swa_out")


def _mixer_dsa(xf, pro, gate, w_in, qlat_g, kvlat_g, w_uq, w_ukv, w_idx_q, qn_g, kn_g, w_out,
               *, batch, seq):
    qrank, kvrank = qlat_g.shape[0], kvlat_g.shape[0]
    hd_dim = kn_g.shape[0]
    n_heads = w_uq.shape[1] // hd_dim
    n_in = w_in.shape[1]
    n_idx = 16
    idx_dim = n_in - qrank - kvrank - n_idx
    assert idx_dim + n_idx <= 128 and (qrank + kvrank) % 128 == 0
    pad = qrank + kvrank + 128 - n_in
    w_in_p = jnp.pad(w_in, ((0, 0), (0, pad))).astype(BF)
    gs = np.concatenate([np.full(qrank, qrank), np.full(kvrank, kvrank), np.zeros(128)]).astype(np.int64)
    gain = jnp.concatenate([qlat_g, kvlat_g, jnp.ones((128,), F32)])
    proj = _matmul(xf, w_in_p, seq=seq, tm=512, tn=qrank + kvrank + 128, out_dtype=BF, prologue=pro,
                   gnorm=(gs, gain), name="dsa_in")
    w_q = jnp.concatenate([w_uq, w_idx_q], axis=1).astype(BF)
    nq_cols, ni_cols = w_uq.shape[1], w_idx_q.shape[1]
    gs_q = np.concatenate([np.full(nq_cols, hd_dim), np.zeros(ni_cols)]).astype(np.int64)
    gain_q = jnp.concatenate([jnp.tile(qn_g * (hd_dim ** -0.5 * LOG2E), n_heads), jnp.ones((ni_cols,), F32)])
    qq = _matmul(proj, w_q, seq=seq, tm=1024, tn=512, out_dtype=BF, lhs_cols=(0, qrank),
                 gnorm=(gs_q, gain_q), name="dsa_q")
    gs_kv = np.concatenate([np.full(hd_dim, hd_dim), np.zeros(hd_dim)]).astype(np.int64)
    gain_kv = jnp.concatenate([kn_g, jnp.ones((hd_dim,), F32)])
    kv = _matmul(proj, w_ukv.astype(BF), seq=seq, tm=1024, tn=2 * hd_dim, out_dtype=BF,
                 lhs_cols=(qrank // kvrank, kvrank), gnorm=(gs_kv, gain_kv), name="dsa_kv")
    o = _dsa_attention(qq, proj, kv, batch=batch, seq=seq, n_heads=n_heads, hd_dim=hd_dim,
                       n_idx=n_idx, idx_dim=idx_dim, kidx_block=(qrank + kvrank) // 128)
    return _matmul(o, w_out.astype(BF), seq=seq, tm=1024, tn=512, out_dtype=F32,
                   resid=(xf, gate), name="dsa_out")


def _mixer_dil(xf, pro, gate, w_in, qn_g, kn_g, w_out, *, batch, seq):
    n_groups = len(DIL_PATTERNS)
    n_heads = w_out.shape[0] // HEAD_DIM
    n_kv = n_heads // 4
    dq, dkv = n_heads * HEAD_DIM, n_kv * HEAD_DIM
    gs = np.tile(np.concatenate([np.full(dq + dkv, HEAD_DIM), np.zeros(dkv)]), n_groups).astype(np.int64)
    gain = jnp.concatenate([
        jnp.concatenate([jnp.tile(qn_g[g] * (HEAD_DIM ** -0.5 * LOG2E), n_heads), jnp.tile(kn_g[g], n_kv),
                         jnp.ones((dkv,), F32)]) for g in range(n_groups)])
    proj = _matmul(xf, w_in.astype(BF), seq=seq, tm=1024, tn=512, out_dtype=BF, prologue=pro,
                   gnorm=(gs, gain), name="dil_in")
    o = _dil_attention(proj, batch=batch, seq=seq, n_heads=n_heads, n_kv=n_kv)
    return _matmul(o, w_out.astype(BF), seq=seq, tm=1024, tn=512, out_dtype=F32,
                   resid=(xf, gate), name="dil_out")


def _mlp(xf, pro, gate, w1, w2, *, seq):
    u = _matmul(xf, w1.astype(BF), seq=seq, tm=1024, tn=512, out_dtype=BF, prologue=pro,
                relu2=True, name="mlp_up")
    return _matmul_ktiled_resid(u, w2.astype(BF), xf, gate, seq=seq, tm=1024, tn=1024, tk=2048,
                                name="mlp_down")


def kernel(x, c, ada_w, ada_b, norm1_g, norm2_g, mlp_w1, mlp_w2, swa_w_in, swa_qn_g, swa_kn_g, swa_sinks, swa_w_out, dsa_w_in, dsa_qlat_g, dsa_kvlat_g, dsa_w_uq, dsa_w_ukv, dsa_w_idx_q, dsa_qn_g, dsa_kn_g, dsa_w_out, dil_w_in, dil_qn_g, dil_kn_g, dil_w_out):
    batch, seq, d = x.shape
    depth = ada_w.shape[0]
    mod = _adaln_mod(c, ada_w, ada_b)
    xf = x.reshape(batch * seq, d)
    for i in range(depth):
        m = mod[i].reshape(batch, 1, 6 * d)
        sh1, sc1, g1, sh2, sc2, g2 = [m[:, :, q * d:(q + 1) * d] for q in range(6)]
        pro = (norm1_g[i], sh1, sc1)
        kind, j = i % 3, i // 3
        if kind == 0:
            xf = _mixer_swa(xf, pro, g1, swa_w_in[j], swa_qn_g[j], swa_kn_g[j], swa_sinks[j],
                            swa_w_out[j], batch=batch, seq=seq)
        elif kind == 1:
            xf = _mixer_dsa(xf, pro, g1, dsa_w_in[j], dsa_qlat_g[j], dsa_kvlat_g[j], dsa_w_uq[j],
                            dsa_w_ukv[j], dsa_w_idx_q[j], dsa_qn_g[j], dsa_kn_g[j], dsa_w_out[j],
                            batch=batch, seq=seq)
        else:
            xf = _mixer_dil(xf, pro, g1, dil_w_in[j], dil_qn_g[j], dil_kn_g[j], dil_w_out[j],
                            batch=batch, seq=seq)
        xf = _mlp(xf, (norm2_g[i], sh2, sc2), g2, mlp_w1[i], mlp_w2[i], seq=seq)
    return xf.reshape(batch, seq, d)
```

```python
import functools

import jax
import jax.numpy as jnp
import numpy as np
from jax import lax
from jax.experimental import pallas as pl
from jax.experimental.pallas import tpu as pltpu

F32 = jnp.float32
BF = jnp.bfloat16
NT = (((1,), (1,)), ((), ()))

EPS = 1e-6
NEG = -1e30
BIG = 1e30
LOG2E = 1.4426950408889634
BLK = 128
HEAD_DIM = 64
TOPK_MAX = 256
DIL_PATTERNS = ((128, 1), (512, 4), (2048, 16))
VMEM_LIMIT = 56 * 1024 * 1024


def _alibi_slopes(n):
    return [float(2.0 ** (-8.0 * (i + 1) / n)) * LOG2E for i in range(n)]


def _cparams(sem):
    return pltpu.CompilerParams(dimension_semantics=sem, vmem_limit_bytes=VMEM_LIMIT)


def _mod_kernel(c_ref, w_ref, b_ref, o_ref):
    c = c_ref[...]
    cond = c * (1.0 / (1.0 + jnp.exp(-c)))
    o_ref[0] = jnp.dot(cond.astype(BF), w_ref[0].astype(BF),
                       preferred_element_type=F32) + b_ref[0]


def _adaln_mod(c, ada_w, ada_b):
    depth, d, n = ada_w.shape
    b = c.shape[0]
    tn = 1024
    return pl.pallas_call(
        _mod_kernel,
        out_shape=jax.ShapeDtypeStruct((depth, b, n), F32),
        grid=(depth, n // tn),
        in_specs=[pl.BlockSpec((b, d), lambda l, j: (0, 0)),
                  pl.BlockSpec((1, d, tn), lambda l, j: (l, 0, j)),
                  pl.BlockSpec((1, 1, tn), lambda l, j: (l, 0, j))],
        out_specs=pl.BlockSpec((1, b, tn), lambda l, j: (l, 0, j)),
        compiler_params=_cparams(("parallel", "parallel")),
        name="adaln_mod",
    )(c, ada_w, ada_b.reshape(depth, 1, n))


def _mm_kernel(*refs, prologue, epilogue, tm):
    refs = list(refs)
    lhs_ref = refs.pop(0)
    if prologue:
        ng_ref, sh_ref, sc_ref = refs.pop(0), refs.pop(0), refs.pop(0)
    w_ref = refs.pop(0)
    if epilogue == "gnorm":
        gm_ref, gain_ref, flag_ref = refs.pop(0), refs.pop(0), refs.pop(0)
    o_ref = refs.pop(0)

    if prologue:
        h_scr = refs.pop(0)
        rows = min(tm, 256)

        @pl.when(pl.program_id(1) == 0)
        def _():
            gmul = ng_ref[...] * (1.0 + sc_ref[0])
            shift = sh_ref[0]

            def body(rc, carry):
                r0 = pl.multiple_of(rc * rows, rows)
                xb = lhs_ref[pl.ds(r0, rows), :]
                ms = jnp.mean(xb * xb, axis=-1, keepdims=True)
                h = xb * lax.rsqrt(ms + EPS) * gmul + shift
                h_scr[pl.ds(r0, rows), :] = h.astype(BF)
                return carry

            lax.fori_loop(0, tm // rows, body, 0)

        lhs = h_scr[...]
    else:
        lhs = lhs_ref[...]

    acc = jnp.dot(lhs, w_ref[0].astype(BF), preferred_element_type=F32)
    if epilogue == "gnorm":
        ms = jnp.dot((acc * acc).astype(BF), gm_ref[0], preferred_element_type=F32)
        scale = jnp.where(flag_ref[...] > 0.0, lax.rsqrt(ms + EPS) * gain_ref[...], 1.0)
        out = acc * scale
    elif epilogue == "relu2":
        r = jnp.maximum(acc, 0.0)
        out = r * r
    else:
        out = acc
    o_ref[...] = out.astype(o_ref.dtype)


def _group_mats(gs, tn):
    n = gs.shape[0]
    col = np.arange(n)
    gsafe = np.maximum(gs, 1)
    grp = col // gsafe
    r = np.arange(tn)
    out = np.zeros((n // tn, tn, tn), np.float32)
    for j in range(n // tn):
        cj = col[j * tn:(j + 1) * tn]
        same = (grp[cj][None, :] == ((r[:, None] + j * tn) // gsafe[cj][None, :]))
        val = np.where(gs[cj] > 0, 1.0 / gsafe[cj], 0.0)[None, :]
        out[j] = np.where(same, val, 0.0)
    return jnp.asarray(out, dtype=BF)


def _matmul(lhs, w, wl, *, seq, tm, tn, out_dtype, lhs_cols=None, prologue=None,
            gnorm=None, relu2=False, name="mm"):
    t = lhs.shape[0]
    _, k, n = w.shape
    tm = min(tm, seq)
    cb = 0 if lhs_cols is None else lhs_cols[0]
    nb = seq // tm
    args, specs = [lhs], [pl.BlockSpec((tm, k), lambda i, j: (i, cb))]
    if prologue is not None:
        ng, sh, sc = prologue
        args += [ng.reshape(1, k), sh, sc]
        specs += [pl.BlockSpec((1, k), lambda i, j: (0, 0)),
                  pl.BlockSpec((1, 1, k), lambda i, j: (i // nb, 0, 0)),
                  pl.BlockSpec((1, 1, k), lambda i, j: (i // nb, 0, 0))]
    args.append(w)
    specs.append(pl.BlockSpec((1, k, tn), lambda i, j: (wl, 0, j)))
    epilogue = "none"
    if gnorm is not None:
        epilogue = "gnorm"
        gs, gain = gnorm
        flag = jnp.asarray((gs > 0).astype(np.float32)).reshape(1, n)
        args += [_group_mats(gs, tn), gain.reshape(1, n).astype(F32), flag]
        specs += [pl.BlockSpec((1, tn, tn), lambda i, j: (j, 0, 0)),
                  pl.BlockSpec((1, tn), lambda i, j: (0, j)),
                  pl.BlockSpec((1, tn), lambda i, j: (0, j))]
    if relu2:
        epilogue = "relu2"
    scratch = [pltpu.VMEM((tm, k), BF)] if prologue is not None else []
    return pl.pallas_call(
        functools.partial(_mm_kernel, prologue=prologue is not None, epilogue=epilogue, tm=tm),
        out_shape=jax.ShapeDtypeStruct((t, n), out_dtype),
        grid=(t // tm, n // tn),
        in_specs=specs,
        out_specs=pl.BlockSpec((tm, tn), lambda i, j: (i, j)),
        scratch_shapes=scratch,
        compiler_params=_cparams(("parallel", "arbitrary")),
        name=name,
    )(*args)


def _out_kernel(*refs, n_mix):
    refs = list(refs)
    if n_mix:
        o_refs = [refs.pop(0) for _ in range(n_mix)]
        l_refs = [refs.pop(0) for _ in range(n_mix)]
        e_ref = refs.pop(0)
    else:
        lhs_ref = refs.pop(0)
    w_ref, res_ref, gate_ref, out_ref = refs
    if n_mix:
        ls = [r[...] for r in l_refs]
        mx = functools.reduce(jnp.maximum, ls)
        es = [jnp.exp2(l - mx) for l in ls]
        inv = 1.0 / functools.reduce(lambda a, b: a + b, es)
        e = e_ref[...]
        lhs = None
        for o_ref, eg in zip(o_refs, es):
            wg = eg * inv
            hi = wg.astype(BF)
            lo = (wg - hi.astype(F32)).astype(BF)
            wfull = (jnp.dot(hi, e, preferred_element_type=F32)
                     + jnp.dot(lo, e, preferred_element_type=F32))
            term = wfull * o_ref[...].astype(F32)
            lhs = term if lhs is None else lhs + term
        lhs = lhs.astype(BF)
    else:
        lhs = lhs_ref[...]
    acc = jnp.dot(lhs, w_ref[0].astype(BF), preferred_element_type=F32)
    out_ref[...] = res_ref[...] + gate_ref[0] * acc


def _out_proj(lhs, w, wl, res, gate, *, seq, tm, mix=None, name):
    _, k, n = w.shape
    t = res.shape[0]
    tm = min(tm, seq)
    nb = seq // tm
    row = lambda width: pl.BlockSpec((tm, width), lambda i: (i, 0))
    if mix is not None:
        outs, lses = mix
        heads = k // HEAD_DIM
        e = np.zeros((BLK, k), np.float32)
        e[np.arange(k) // HEAD_DIM, np.arange(k)] = 1.0
        assert heads <= BLK
        args = list(outs) + list(lses) + [jnp.asarray(e, dtype=BF)]
        specs = [row(k)] * len(outs) + [row(BLK)] * len(lses) + [pl.BlockSpec((BLK, k), lambda i: (0, 0))]
        n_mix = len(outs)
    else:
        args, specs, n_mix = [lhs], [row(k)], 0
    args += [w, res, gate]
    specs += [pl.BlockSpec((1, k, n), lambda i: (wl, 0, 0)),
              row(n),
              pl.BlockSpec((1, 1, n), lambda i: (i // nb, 0, 0))]
    return pl.pallas_call(
        functools.partial(_out_kernel, n_mix=n_mix),
        out_shape=jax.ShapeDtypeStruct((t, n), F32),
        grid=(t // tm,),
        in_specs=specs,
        out_specs=row(n),
        compiler_params=_cparams(("parallel",)),
        name=name,
    )(*args)


def _mmk_kernel(a_ref, w_ref, res_ref, gate_ref, o_ref, acc_ref):
    kk = pl.program_id(2)

    @pl.when(kk == 0)
    def _():
        acc_ref[...] = jnp.zeros_like(acc_ref)

    acc_ref[...] += jnp.dot(a_ref[...], w_ref[0], preferred_element_type=F32)

    @pl.when(kk == pl.num_programs(2) - 1)
    def _():
        o_ref[...] = res_ref[...] + gate_ref[0] * acc_ref[...]


def _matmul_ktiled_resid(a, w, wl, res, gate, *, seq, tm, tn, tk, name):
    t, k = a.shape
    n = w.shape[2]
    tm = min(tm, seq)
    nb = seq // tm
    return pl.pallas_call(
        _mmk_kernel,
        out_shape=jax.ShapeDtypeStruct((t, n), F32),
        grid=(t // tm, n // tn, k // tk),
        in_specs=[pl.BlockSpec((tm, tk), lambda i, j, q: (i, q)),
                  pl.BlockSpec((1, tk, tn), lambda i, j, q: (wl, q, j)),
                  pl.BlockSpec((tm, tn), lambda i, j, q: (i, j)),
                  pl.BlockSpec((1, 1, tn), lambda i, j, q: (i // nb, 0, j))],
        out_specs=pl.BlockSpec((tm, tn), lambda i, j, q: (i, j)),
        scratch_shapes=[pltpu.VMEM((tm, tn), F32)],
        compiler_params=_cparams(("parallel", "parallel", "arbitrary")),
        name=name,
    )(a, w, res, gate)


def _split_heads_k(tile, odd):
    nk = tile.shape[0]
    z = jnp.zeros((nk, HEAD_DIM), tile.dtype)
    half = tile[:, HEAD_DIM:] if odd else tile[:, :HEAD_DIM]
    return jnp.concatenate([half, z], axis=1), jnp.concatenate([z, half], axis=1)


def _band_kernel(*refs, slopes, n_kv, group, dil, inclusive, has_sink, has_prev, with_lse,
                 q_width, kv_shared, v_off):
    refs = list(refs)
    sink_ref = refs.pop(0) if has_sink else None
    n_heads = n_kv * group
    q_refs = [refs.pop(0) for _ in range(n_heads * HEAD_DIM // q_width)]
    n_blk = 2 if has_prev else 1
    k_refs = [refs.pop(0) for _ in range(n_blk)]
    v_refs = k_refs if kv_shared else [refs.pop(0) for _ in range(n_blk)]
    o_ref = refs.pop(0)
    if with_lse:
        lse_ref, lse_scr = refs.pop(0), refs.pop(0)
        lse_scr[...] = jnp.zeros(lse_scr.shape, F32)
    a = pl.program_id(2)
    dkv = n_kv * HEAD_DIM
    k = jnp.concatenate([r[0, :, 0:dkv] for r in k_refs], axis=0)
    v = jnp.concatenate([r[0, :, v_off:v_off + dkv] for r in v_refs], axis=0)
    vt = v.T
    nk = n_blk * BLK
    row = lax.broadcasted_iota(jnp.int32, (nk, BLK), 0)
    col = lax.broadcasted_iota(jnp.int32, (nk, BLK), 1)
    da = col - row + (nk - BLK)
    near = (da <= BLK) if inclusive else (da < BLK)
    if has_prev:
        exists = jnp.where(row >= BLK, 1, jnp.where(a > 0, 1, 0))
    else:
        exists = 1
    ok = jnp.where(da >= 0, jnp.where(near, exists, 0), 0)
    dm = jnp.where(ok > 0, (da * dil).astype(F32), BIG)
    pairs = group // 2
    tiles_per_ref = q_width // BLK
    for kv in range(n_kv):
        ka, kb = _split_heads_k(k[:, (kv // 2) * BLK:(kv // 2 + 1) * BLK], kv % 2 == 1)
        tiles = []
        for p_ in range(pairs):
            tq = kv * pairs + p_
            cq = (tq % tiles_per_ref) * BLK
            tiles.append(q_refs[tq // tiles_per_ref][0, :, cq:cq + BLK])
        qst = jnp.concatenate(tiles, axis=0)
        s_even = lax.dot_general(ka, qst, NT, preferred_element_type=F32)
        s_odd = lax.dot_general(kb, qst, NT, preferred_element_type=F32)
        pts, invs = [], []
        for g in range(group):
            p_, odd = divmod(g, 2)
            h = kv * group + g
            s = (s_odd if odd else s_even)[:, p_ * BLK:(p_ + 1) * BLK] - slopes[h] * dm
            m = jnp.max(s, axis=0, keepdims=True)
            if has_sink:
                sink = sink_ref[h] * LOG2E
                m = jnp.maximum(m, sink)
            p = jnp.exp2(s - m)
            den = jnp.sum(p, axis=0, keepdims=True)
            if has_sink:
                den = den + jnp.exp2(sink - m)
            if with_lse:
                lse_scr[h:h + 1, :] = m + jnp.log2(den)
            pts.append(p.astype(BF))
            invs.append(1.0 / den)
        acc = jnp.dot(vt[kv * HEAD_DIM:(kv + 1) * HEAD_DIM, :], jnp.concatenate(pts, axis=1),
                      preferred_element_type=F32)
        for p_ in range(pairs):
            oe = acc[:, (2 * p_) * BLK:(2 * p_ + 1) * BLK] * invs[2 * p_]
            oo = acc[:, (2 * p_ + 1) * BLK:(2 * p_ + 2) * BLK] * invs[2 * p_ + 1]
            c0 = (kv * pairs + p_) * BLK
            o_ref[0, :, c0:c0 + BLK] = jnp.concatenate([oe, oo], axis=0).T.astype(o_ref.dtype)
    if with_lse:
        lse_ref[0] = lse_scr[...].T


def _band_attention(src, *, batch, seq, row_cols, q_col, kv_col, n_heads, n_kv, dil, inclusive,
                    sinks=None, with_lse=False, name):
    cls = seq // dil
    assert seq % dil == 0 and cls % BLK == 0
    n_a = cls // BLK
    dq, dkv = n_heads * HEAD_DIM, n_kv * HEAD_DIM
    view = src.reshape(batch, cls, dil * row_cols)
    has_prev = n_a > 1
    kv_shared = (2 * dkv) <= 512 and kv_col % (2 * dkv) == 0 and row_cols % (2 * dkv) == 0
    q_width = 512 if (q_col % dq or row_cols % dq) else dq
    assert q_col % q_width == 0 and row_cols % q_width == 0 and dq % q_width == 0

    def spec(width, col0, prev):
        cb = col0 // width
        per = row_cols // width
        if prev:
            return pl.BlockSpec((1, BLK, width), lambda b, r, a: (b, jnp.maximum(a - 1, 0), r * per + cb))
        return pl.BlockSpec((1, BLK, width), lambda b, r, a: (b, a, r * per + cb))

    args, specs = [], []
    if sinks is not None:
        args.append(sinks.astype(F32))
        specs.append(pl.BlockSpec(memory_space=pltpu.SMEM))
    for j in range(dq // q_width):
        args.append(view)
        specs.append(spec(q_width, q_col + j * q_width, False))
    blocks = [True, False] if has_prev else [False]
    if kv_shared:
        for prev in blocks:
            args.append(view)
            specs.append(spec(2 * dkv, kv_col, prev))
        v_off = dkv
    else:
        for col0 in (kv_col, kv_col + dkv):
            for prev in blocks:
                args.append(view)
                specs.append(spec(dkv, col0, prev))
        v_off = 0
    out_shape = [jax.ShapeDtypeStruct((batch, cls, dil * dq), BF)]
    out_specs = [pl.BlockSpec((1, BLK, dq), lambda b, r, a: (b, a, r))]
    scratch = []
    if with_lse:
        out_shape.append(jax.ShapeDtypeStruct((batch, cls, dil * BLK), F32))
        out_specs.append(pl.BlockSpec((1, BLK, BLK), lambda b, r, a: (b, a, r)))
        scratch.append(pltpu.VMEM((BLK, BLK), F32))
    res = pl.pallas_call(
        functools.partial(_band_kernel, slopes=_alibi_slopes(n_heads), n_kv=n_kv, group=n_heads // n_kv,
                          dil=dil, inclusive=inclusive, has_sink=sinks is not None, has_prev=has_prev,
                          with_lse=with_lse, q_width=q_width, kv_shared=kv_shared, v_off=v_off),
        out_shape=out_shape,
        grid=(batch, dil, n_a),
        in_specs=specs,
        out_specs=out_specs,
        scratch_shapes=scratch,
        compiler_params=_cparams(("parallel", "parallel", "parallel")),
        name=name,
    )(*args)
    o = res[0].reshape(batch * seq, dq)
    if with_lse:
        return o, res[1].reshape(batch * seq, BLK)
    return o


def _dsa_kernel(q_ref, qi_ref, wq_ref, kv_ref, ki_ref, o_ref,
                vt_scr, ka_scr, kb_scr, qst_scr, qist_scr, sc_scr, m_scr, l_scr, acc_scr,
                *, slopes, topk, ch, n_heads, n_idx, hd_dim, idx_dim):
    i = pl.program_id(1)
    t0 = i * BLK
    nch = (t0 + BLK - 1) // ch + 1

    @pl.when(i == 0)
    def _():
        vt_scr[...] = kv_ref[:, hd_dim:2 * hd_dim].T
        ka, kb = _split_heads_k(ki_ref[...], False)
        ka_scr[...] = ka
        kb_scr[...] = kb

    for h in range(n_heads):
        qst_scr[h * BLK:(h + 1) * BLK, :] = q_ref[:, h * hd_dim:(h + 1) * hd_dim]
    for p in range(n_idx // 2):
        qist_scr[p * BLK:(p + 1) * BLK, :] = qi_ref[:, p * BLK:(p + 1) * BLK]

    wscale = float(idx_dim ** -0.5) * float(n_idx ** -0.5)
    wt = wq_ref[...].astype(F32).T * wscale
    row = lax.broadcasted_iota(jnp.int32, (ch, BLK), 0)
    col = lax.broadcasted_iota(jnp.int32, (ch, BLK), 1)
    cr = col - row

    def p1(cc, carry):
        start = pl.multiple_of(cc * ch, ch)
        qi = qist_scr[...]
        rel_e = lax.dot_general(ka_scr[pl.ds(start, ch), :], qi, NT, preferred_element_type=F32)
        rel_o = lax.dot_general(kb_scr[pl.ds(start, ch), :], qi, NT, preferred_element_type=F32)
        score = jnp.zeros((ch, BLK), F32)
        for p in range(n_idx // 2):
            we = wt[idx_dim + 2 * p:idx_dim + 2 * p + 1, :]
            wo = wt[idx_dim + 2 * p + 1:idx_dim + 2 * p + 2, :]
            score = score + we * jnp.maximum(rel_e[:, p * BLK:(p + 1) * BLK], 0.0)
            score = score + wo * jnp.maximum(rel_o[:, p * BLK:(p + 1) * BLK], 0.0)
        sc_scr[pl.ds(start, ch), :] = jnp.where(cr + (t0 - start) >= 0, score, -jnp.inf)
        return carry

    lax.fori_loop(0, nch, p1, 0)

    tq = t0 + lax.broadcasted_iota(jnp.int32, (1, BLK), 1)
    kq = jnp.minimum(tq + 1, topk).astype(F32)

    def key_to_f32(key):
        bits = jnp.where(key < 0, key ^ jnp.int32(0x7FFFFFFF), key)
        return lax.bitcast_convert_type(bits, F32)

    def count_ge(cf):
        lanes = 64

        def body(cc, acc):
            blk = sc_scr[pl.ds(pl.multiple_of(cc * ch, ch), ch), :]
            hit = jnp.where(blk >= cf, 1.0, 0.0)
            for j in range(ch // lanes):
                acc = acc + hit[j * lanes:(j + 1) * lanes, :]
            return acc

        acc = lax.fori_loop(0, nch, body, jnp.zeros((lanes, BLK), F32))
        return jnp.sum(acc, axis=0, keepdims=True)

    ans0 = jnp.where(count_ge(jnp.zeros((1, BLK), F32)) >= kq,
                     jnp.int32(0), jnp.int32(-2147483648))

    def radix(b, ans):
        cand = ans | lax.shift_left(jnp.int32(1), 30 - b)
        return jnp.where(count_ge(key_to_f32(cand)) >= kq, cand, ans)

    thr = key_to_f32(lax.fori_loop(0, 31, radix, ans0))

    def p3(cc, carry):
        start = pl.multiple_of(cc * ch, ch)
        sel = sc_scr[pl.ds(start, ch), :] >= thr
        sc_scr[pl.ds(start, ch), :] = jnp.where(sel, (cr + (t0 - start)).astype(F32), BIG)
        return carry

    lax.fori_loop(0, nch, p3, 0)

    m_scr[...] = jnp.full(m_scr.shape, NEG, F32)
    l_scr[...] = jnp.zeros(l_scr.shape, F32)
    acc_scr[...] = jnp.zeros(acc_scr.shape, F32)

    def p4(cc, carry):
        start = pl.multiple_of(cc * ch, ch)
        s_all = lax.dot_general(kv_ref[pl.ds(start, ch), 0:hd_dim], qst_scr[...], NT,
                                preferred_element_type=F32)
        dm = sc_scr[pl.ds(start, ch), :]
        pts, alphas = [], []
        for h in range(n_heads):
            hs = slice(h * BLK, (h + 1) * BLK)
            s = s_all[:, hs] - slopes[h] * dm
            m_old = m_scr[:, hs]
            m_new = jnp.maximum(m_old, jnp.max(s, axis=0, keepdims=True))
            alpha = jnp.exp2(m_old - m_new)
            p = jnp.exp2(s - m_new)
            l_scr[:, hs] = alpha * l_scr[:, hs] + jnp.sum(p, axis=0, keepdims=True)
            m_scr[:, hs] = m_new
            pts.append(p.astype(BF))
            alphas.append(alpha)
        pv = jnp.dot(vt_scr[:, pl.ds(start, ch)], jnp.concatenate(pts, axis=1),
                     preferred_element_type=F32)
        acc_scr[...] = jnp.concatenate(alphas, axis=1) * acc_scr[...] + pv
        return carry

    lax.fori_loop(0, nch, p4, 0)
    for h in range(n_heads):
        hs = slice(h * BLK, (h + 1) * BLK)
        o_ref[:, h * hd_dim:(h + 1) * hd_dim] = (acc_scr[:, hs] * (1.0 / l_scr[:, hs])).T.astype(o_ref.dtype)


def _dsa_attention(qq, proj, kv, *, batch, seq, n_heads, hd_dim, n_idx, idx_dim, kidx_block):
    t = qq.shape[0]
    nq = seq // BLK
    dq = n_heads * hd_dim
    di = n_idx * idx_dim
    assert dq % di == 0 and hd_dim == BLK and idx_dim == HEAD_DIM
    topk = min(TOPK_MAX, seq // 4)
    ch = min(512, seq)
    return pl.pallas_call(
        functools.partial(_dsa_kernel, slopes=_alibi_slopes(n_heads), topk=topk, ch=ch,
                          n_heads=n_heads, n_idx=n_idx, hd_dim=hd_dim, idx_dim=idx_dim),
        out_shape=jax.ShapeDtypeStruct((t, dq), BF),
        grid=(batch, nq),
        in_specs=[pl.BlockSpec((BLK, dq), lambda b, i: (b * nq + i, 0)),
                  pl.BlockSpec((BLK, di), lambda b, i: (b * nq + i, dq // di)),
                  pl.BlockSpec((BLK, BLK), lambda b, i: (b * nq + i, kidx_block)),
                  pl.BlockSpec((seq, 2 * hd_dim), lambda b, i: (b, 0)),
                  pl.BlockSpec((seq, BLK), lambda b, i: (b, kidx_block))],
        out_specs=pl.BlockSpec((BLK, dq), lambda b, i: (b * nq + i, 0)),
        scratch_shapes=[pltpu.VMEM((hd_dim, seq), BF),
                        pltpu.VMEM((seq, BLK), BF),
                        pltpu.VMEM((seq, BLK), BF),
                        pltpu.VMEM((n_heads * BLK, hd_dim), BF),
                        pltpu.VMEM((n_idx // 2 * BLK, BLK), BF),
                        pltpu.VMEM((seq, BLK), F32),
                        pltpu.VMEM((1, n_heads * BLK), F32),
                        pltpu.VMEM((1, n_heads * BLK), F32),
                        pltpu.VMEM((hd_dim, n_heads * BLK), F32)],
        compiler_params=_cparams(("parallel", "arbitrary")),
        name="dsa_attn",
    )(qq, qq, proj, kv, proj)


def _mixer_swa(xf, pro, gate, w_in, w_out, j, qn_g, kn_g, sinks, *, batch, seq):
    d = xf.shape[1]
    n_heads = d // HEAD_DIM
    n_kv = n_heads // 8
    dq, dkv = n_heads * HEAD_DIM, n_kv * HEAD_DIM
    gs = np.concatenate([np.full(dq + dkv, HEAD_DIM), np.zeros(dkv)]).astype(np.int64)
    gain = jnp.concatenate([jnp.tile(qn_g * (HEAD_DIM ** -0.5 * LOG2E), n_heads), jnp.tile(kn_g, n_kv),
                            jnp.ones((dkv,), F32)])
    qkv = _matmul(xf, w_in, j, seq=seq, tm=1024, tn=256, out_dtype=BF, prologue=pro,
                  gnorm=(gs, gain), name="swa_in")
    o = _band_attention(qkv, batch=batch, seq=seq, row_cols=dq + 2 * dkv, q_col=0, kv_col=dq,
                        n_heads=n_heads, n_kv=n_kv, dil=1, inclusive=False, sinks=sinks, name="swa_attn")
    return _out_proj(o, w_out, j, xf, gate, seq=seq, tm=512, name="swa_out")


def _mixer_dsa(xf, pro, gate, w_in, qlat_g, kvlat_g, w_uq, w_ukv, w_idx_q, qn_g, kn_g, w_out, j,
               *, batch, seq):
    qrank, kvrank = qlat_g.shape[0], kvlat_g.shape[0]
    hd_dim = kn_g.shape[0]
    n_heads = w_uq.shape[1] // hd_dim
    n_in = w_in.shape[1]
    n_idx = 16
    idx_dim = n_in - qrank - kvrank - n_idx
    assert idx_dim + n_idx <= 128 and (qrank + kvrank) % 128 == 0
    pad = qrank + kvrank + 128 - n_in
    w_in_p = jnp.pad(w_in, ((0, 0), (0, pad))).astype(BF)[None]
    gs = np.concatenate([np.full(qrank, qrank), np.full(kvrank, kvrank), np.zeros(128)]).astype(np.int64)
    gain = jnp.concatenate([qlat_g, kvlat_g, jnp.ones((128,), F32)])
    proj = _matmul(xf, w_in_p, 0, seq=seq, tm=512, tn=qrank + kvrank + 128, out_dtype=BF, prologue=pro,
                   gnorm=(gs, gain), name="dsa_in")
    w_q = jnp.concatenate([w_uq, w_idx_q], axis=1).astype(BF)[None]
    nq_cols, ni_cols = w_uq.shape[1], w_idx_q.shape[1]
    gs_q = np.concatenate([np.full(nq_cols, hd_dim), np.zeros(ni_cols)]).astype(np.int64)
    gain_q = jnp.concatenate([jnp.tile(qn_g * (hd_dim ** -0.5 * LOG2E), n_heads), jnp.ones((ni_cols,), F32)])
    qq = _matmul(proj, w_q, 0, seq=seq, tm=1024, tn=256, out_dtype=BF, lhs_cols=(0, qrank),
                 gnorm=(gs_q, gain_q), name="dsa_q")
    gs_kv = np.concatenate([np.full(hd_dim, hd_dim), np.zeros(hd_dim)]).astype(np.int64)
    gain_kv = jnp.concatenate([kn_g, jnp.ones((hd_dim,), F32)])
    kv = _matmul(proj, w_ukv[None], 0, seq=seq, tm=1024, tn=2 * hd_dim, out_dtype=BF,
                 lhs_cols=(qrank // kvrank, kvrank), gnorm=(gs_kv, gain_kv), name="dsa_kv")
    o = _dsa_attention(qq, proj, kv, batch=batch, seq=seq, n_heads=n_heads, hd_dim=hd_dim,
                       n_idx=n_idx, idx_dim=idx_dim, kidx_block=(qrank + kvrank) // 128)
    return _out_proj(o, w_out, j, xf, gate, seq=seq, tm=512, name="dsa_out")


def _mixer_dil(xf, pro, gate, w_in, w_out, j, qn_g, kn_g, *, batch, seq):
    n_groups = len(DIL_PATTERNS)
    n_heads = w_out.shape[1] // HEAD_DIM
    n_kv = n_heads // 4
    dq, dkv = n_heads * HEAD_DIM, n_kv * HEAD_DIM
    gcols = dq + 2 * dkv
    gs = np.tile(np.concatenate([np.full(dq + dkv, HEAD_DIM), np.zeros(dkv)]), n_groups).astype(np.int64)
    gain = jnp.concatenate([
        jnp.concatenate([jnp.tile(qn_g[g] * (HEAD_DIM ** -0.5 * LOG2E), n_heads), jnp.tile(kn_g[g], n_kv),
                         jnp.ones((dkv,), F32)]) for g in range(n_groups)])
    proj = _matmul(xf, w_in, j, seq=seq, tm=1024, tn=256, out_dtype=BF, prologue=pro,
                   gnorm=(gs, gain), name="dil_in")
    outs, lses = [], []
    for g, (win, dil) in enumerate(DIL_PATTERNS):
        assert win == BLK * dil
        o, lse = _band_attention(proj, batch=batch, seq=seq, row_cols=n_groups * gcols, q_col=g * gcols,
                                 kv_col=g * gcols + dq, n_heads=n_heads, n_kv=n_kv, dil=dil,
                                 inclusive=True, with_lse=True, name=f"dil_attn{g}")
        outs.append(o)
        lses.append(lse)
    return _out_proj(None, w_out, j, xf, gate, seq=seq, tm=512, mix=(outs, lses), name="dil_out")


def _mlp(xf, pro, gate, w1, w2, i, *, seq):
    u = _matmul(xf, w1, i, seq=seq, tm=1024, tn=512, out_dtype=BF, prologue=pro,
                relu2=True, name="mlp_up")
    return _matmul_ktiled_resid(u, w2, i, xf, gate, seq=seq, tm=1024, tn=1024, tk=2048,
                                name="mlp_down")


def kernel(x, c, ada_w, ada_b, norm1_g, norm2_g, mlp_w1, mlp_w2, swa_w_in, swa_qn_g, swa_kn_g, swa_sinks, swa_w_out, dsa_w_in, dsa_qlat_g, dsa_kvlat_g, dsa_w_uq, dsa_w_ukv, dsa_w_idx_q, dsa_qn_g, dsa_kn_g, dsa_w_out, dil_w_in, dil_qn_g, dil_kn_g, dil_w_out):
    batch, seq, d = x.shape
    depth = ada_w.shape[0]
    mod = _adaln_mod(c, ada_w, ada_b)
    xf = x.reshape(batch * seq, d)
    mlp_w2_bf = mlp_w2.astype(BF)
    swa_w_out_bf, dsa_w_out_bf, dil_w_out_bf = (w.astype(BF) for w in (swa_w_out, dsa_w_out, dil_w_out))
    for i in range(depth):
        m = mod[i].reshape(batch, 1, 6 * d)
        sh1, sc1, g1, sh2, sc2, g2 = [m[:, :, q * d:(q + 1) * d] for q in range(6)]
        pro = (norm1_g[i], sh1, sc1)
        kind, j = i % 3, i // 3
        if kind == 0:
            xf = _mixer_swa(xf, pro, g1, swa_w_in, swa_w_out_bf, j, swa_qn_g[j], swa_kn_g[j], swa_sinks[j],
                            batch=batch, seq=seq)
        elif kind == 1:
            xf = _mixer_dsa(xf, pro, g1, dsa_w_in[j], dsa_qlat_g[j], dsa_kvlat_g[j], dsa_w_uq[j],
                            dsa_w_ukv[j], dsa_w_idx_q[j], dsa_qn_g[j], dsa_kn_g[j], dsa_w_out_bf, j,
                            batch=batch, seq=seq)
        else:
            xf = _mixer_dil(xf, pro, g1, dil_w_in, dil_w_out_bf, j, dil_qn_g[j], dil_kn_g[j],
                            batch=batch, seq=seq)
        xf = _mlp(xf, (norm2_g[i], sh2, sc2), g2, mlp_w1, mlp_w2_bf, i, seq=seq)
    return xf.reshape(batch, seq, d)
```

```python
import functools

import jax
import jax.numpy as jnp
import numpy as np
from jax import lax
from jax.experimental import pallas as pl
from jax.experimental.pallas import tpu as pltpu

F32 = jnp.float32
BF = jnp.bfloat16
NT = (((1,), (1,)), ((), ()))

EPS = 1e-6
NEG = -1e30
BIG = 1e30
LOG2E = 1.4426950408889634
BLK = 128
HEAD_DIM = 64
TOPK_MAX = 256
DIL_PATTERNS = ((128, 1), (512, 4), (2048, 16))
VMEM_LIMIT = 56 * 1024 * 1024


def _alibi_slopes(n):
    return [float(2.0 ** (-8.0 * (i + 1) / n)) * LOG2E for i in range(n)]


def _cparams(sem):
    return pltpu.CompilerParams(dimension_semantics=sem, vmem_limit_bytes=VMEM_LIMIT)


def _mod_kernel(c_ref, w_ref, b_ref, o_ref):
    c = c_ref[...]
    cond = c * (1.0 / (1.0 + jnp.exp(-c)))
    o_ref[0] = jnp.dot(cond.astype(BF), w_ref[0].astype(BF),
                       preferred_element_type=F32) + b_ref[0]


def _adaln_mod(c, ada_w, ada_b):
    depth, d, n = ada_w.shape
    b = c.shape[0]
    tn = 1024
    return pl.pallas_call(
        _mod_kernel,
        out_shape=jax.ShapeDtypeStruct((depth, b, n), F32),
        grid=(depth, n // tn),
        in_specs=[pl.BlockSpec((b, d), lambda l, j: (0, 0)),
                  pl.BlockSpec((1, d, tn), lambda l, j: (l, 0, j)),
                  pl.BlockSpec((1, 1, tn), lambda l, j: (l, 0, j))],
        out_specs=pl.BlockSpec((1, b, tn), lambda l, j: (l, 0, j)),
        compiler_params=_cparams(("parallel", "parallel")),
        name="adaln_mod",
    )(c, ada_w, ada_b.reshape(depth, 1, n))


def _mm_kernel(*refs, prologue, epilogue, tm, tile_major):
    refs = list(refs)
    lhs_ref = refs.pop(0)
    if prologue:
        ng_ref, sh_ref, sc_ref = refs.pop(0), refs.pop(0), refs.pop(0)
    w_ref = refs.pop(0)
    if epilogue == "gnorm":
        gm_ref, gain_ref, flag_ref = refs.pop(0), refs.pop(0), refs.pop(0)
    o_ref = refs.pop(0)

    if prologue:
        h_scr = refs.pop(0)
        rows = min(tm, 256)

        @pl.when(pl.program_id(1) == 0)
        def _():
            gmul = ng_ref[...] * (1.0 + sc_ref[0])
            shift = sh_ref[0]

            def body(rc, carry):
                r0 = pl.multiple_of(rc * rows, rows)
                xb = lhs_ref[pl.ds(r0, rows), :]
                ms = jnp.mean(xb * xb, axis=-1, keepdims=True)
                h = xb * lax.rsqrt(ms + EPS) * gmul + shift
                h_scr[pl.ds(r0, rows), :] = h.astype(BF)
                return carry

            lax.fori_loop(0, tm // rows, body, 0)

        lhs = h_scr[...]
    else:
        lhs = lhs_ref[...]

    acc = jnp.dot(lhs, w_ref[0].astype(BF), preferred_element_type=F32)
    if epilogue == "gnorm":
        ms = jnp.dot((acc * acc).astype(BF), gm_ref[0], preferred_element_type=F32)
        scale = jnp.where(flag_ref[...] > 0.0, lax.rsqrt(ms + EPS) * gain_ref[...], 1.0)
        out = acc * scale
    elif epilogue == "relu2":
        r = jnp.maximum(acc, 0.0)
        out = r * r
    else:
        out = acc
    if tile_major:
        for cc in range(out.shape[1] // BLK):
            o_ref[cc] = out[:, cc * BLK:(cc + 1) * BLK].astype(o_ref.dtype)
    else:
        o_ref[...] = out.astype(o_ref.dtype)


def _group_mats(gs, tn):
    n = gs.shape[0]
    col = np.arange(n)
    gsafe = np.maximum(gs, 1)
    grp = col // gsafe
    r = np.arange(tn)
    out = np.zeros((n // tn, tn, tn), np.float32)
    for j in range(n // tn):
        cj = col[j * tn:(j + 1) * tn]
        same = (grp[cj][None, :] == ((r[:, None] + j * tn) // gsafe[cj][None, :]))
        val = np.where(gs[cj] > 0, 1.0 / gsafe[cj], 0.0)[None, :]
        out[j] = np.where(same, val, 0.0)
    return jnp.asarray(out, dtype=BF)


def _matmul(lhs, w, wl, *, seq, tm, tn, out_dtype, lhs_cols=None, prologue=None,
            gnorm=None, relu2=False, tile_major=False, name="mm"):
    t = lhs.shape[0]
    _, k, n = w.shape
    tm = min(tm, seq)
    cb = 0 if lhs_cols is None else lhs_cols[0]
    nb = seq // tm
    args, specs = [lhs], [pl.BlockSpec((tm, k), lambda i, j: (i, cb))]
    if prologue is not None:
        ng, sh, sc = prologue
        args += [ng.reshape(1, k), sh, sc]
        specs += [pl.BlockSpec((1, k), lambda i, j: (0, 0)),
                  pl.BlockSpec((1, 1, k), lambda i, j: (i // nb, 0, 0)),
                  pl.BlockSpec((1, 1, k), lambda i, j: (i // nb, 0, 0))]
    args.append(w)
    specs.append(pl.BlockSpec((1, k, tn), lambda i, j: (wl, 0, j)))
    epilogue = "none"
    if gnorm is not None:
        epilogue = "gnorm"
        gs, gain = gnorm
        flag = jnp.asarray((gs > 0).astype(np.float32)).reshape(1, n)
        args += [_group_mats(gs, tn), gain.reshape(1, n).astype(F32), flag]
        specs += [pl.BlockSpec((1, tn, tn), lambda i, j: (j, 0, 0)),
                  pl.BlockSpec((1, tn), lambda i, j: (0, j)),
                  pl.BlockSpec((1, tn), lambda i, j: (0, j))]
    if relu2:
        epilogue = "relu2"
    scratch = [pltpu.VMEM((tm, k), BF)] if prologue is not None else []
    if tile_major:
        out_shape = jax.ShapeDtypeStruct((n // BLK, t, BLK), out_dtype)
        out_spec = pl.BlockSpec((tn // BLK, tm, BLK), lambda i, j: (j, i, 0))
    else:
        out_shape = jax.ShapeDtypeStruct((t, n), out_dtype)
        out_spec = pl.BlockSpec((tm, tn), lambda i, j: (i, j))
    return pl.pallas_call(
        functools.partial(_mm_kernel, prologue=prologue is not None, epilogue=epilogue, tm=tm,
                          tile_major=tile_major),
        out_shape=out_shape,
        grid=(t // tm, n // tn),
        in_specs=specs,
        out_specs=out_spec,
        scratch_shapes=scratch,
        compiler_params=_cparams(("parallel", "arbitrary")),
        name=name,
    )(*args)


def _out_kernel(*refs, n_mix):
    refs = list(refs)
    if n_mix:
        o_refs = [refs.pop(0) for _ in range(n_mix)]
        l_refs = [refs.pop(0) for _ in range(n_mix)]
        e_ref = refs.pop(0)
    else:
        lhs_ref = refs.pop(0)
    w_ref, res_ref, gate_ref, out_ref = refs
    if n_mix:
        ls = [r[...] for r in l_refs]
        mx = functools.reduce(jnp.maximum, ls)
        es = [jnp.exp2(l - mx) for l in ls]
        inv = 1.0 / functools.reduce(lambda a, b: a + b, es)
        e = e_ref[...]
        lhs = None
        for o_ref, eg in zip(o_refs, es):
            wg = eg * inv
            hi = wg.astype(BF)
            lo = (wg - hi.astype(F32)).astype(BF)
            wfull = (jnp.dot(hi, e, preferred_element_type=F32)
                     + jnp.dot(lo, e, preferred_element_type=F32))
            og = jnp.concatenate([o_ref[cc] for cc in range(o_ref.shape[0])], axis=1)
            term = wfull * og
            lhs = term if lhs is None else lhs + term
        lhs = lhs.astype(BF)
    else:
        lhs = lhs_ref[...]
    acc = jnp.dot(lhs, w_ref[0].astype(BF), preferred_element_type=F32)
    out_ref[...] = res_ref[...] + gate_ref[0] * acc


def _out_proj(lhs, w, wl, res, gate, *, seq, tm, mix=None, name):
    _, k, n = w.shape
    t = res.shape[0]
    tm = min(tm, seq)
    nb = seq // tm
    row = lambda width: pl.BlockSpec((tm, width), lambda i: (i, 0))
    if mix is not None:
        outs, lses = mix
        heads = k // HEAD_DIM
        e = np.zeros((BLK, k), np.float32)
        e[np.arange(k) // HEAD_DIM, np.arange(k)] = 1.0
        assert heads <= BLK
        args = list(outs) + list(lses) + [jnp.asarray(e, dtype=BF)]
        tiles = pl.BlockSpec((k // BLK, tm, BLK), lambda i: (0, i, 0))
        specs = [tiles] * len(outs) + [row(BLK)] * len(lses) + [pl.BlockSpec((BLK, k), lambda i: (0, 0))]
        n_mix = len(outs)
    else:
        args, specs, n_mix = [lhs], [row(k)], 0
    args += [w, res, gate]
    specs += [pl.BlockSpec((1, k, n), lambda i: (wl, 0, 0)),
              row(n),
              pl.BlockSpec((1, 1, n), lambda i: (i // nb, 0, 0))]
    return pl.pallas_call(
        functools.partial(_out_kernel, n_mix=n_mix),
        out_shape=jax.ShapeDtypeStruct((t, n), F32),
        grid=(t // tm,),
        in_specs=specs,
        out_specs=row(n),
        compiler_params=_cparams(("parallel",)),
        name=name,
    )(*args)


def _mmk_kernel(a_ref, w_ref, res_ref, gate_ref, o_ref, acc_ref):
    kk = pl.program_id(2)

    @pl.when(kk == 0)
    def _():
        acc_ref[...] = jnp.zeros_like(acc_ref)

    acc_ref[...] += jnp.dot(a_ref[...], w_ref[0], preferred_element_type=F32)

    @pl.when(kk == pl.num_programs(2) - 1)
    def _():
        o_ref[...] = res_ref[...] + gate_ref[0] * acc_ref[...]


def _matmul_ktiled_resid(a, w, wl, res, gate, *, seq, tm, tn, tk, name):
    t, k = a.shape
    n = w.shape[2]
    tm = min(tm, seq)
    nb = seq // tm
    return pl.pallas_call(
        _mmk_kernel,
        out_shape=jax.ShapeDtypeStruct((t, n), F32),
        grid=(t // tm, n // tn, k // tk),
        in_specs=[pl.BlockSpec((tm, tk), lambda i, j, q: (i, q)),
                  pl.BlockSpec((1, tk, tn), lambda i, j, q: (wl, q, j)),
                  pl.BlockSpec((tm, tn), lambda i, j, q: (i, j)),
                  pl.BlockSpec((1, 1, tn), lambda i, j, q: (i // nb, 0, j))],
        out_specs=pl.BlockSpec((tm, tn), lambda i, j, q: (i, j)),
        scratch_shapes=[pltpu.VMEM((tm, tn), F32)],
        compiler_params=_cparams(("parallel", "parallel", "arbitrary")),
        name=name,
    )(a, w, res, gate)


def _split_heads_k(tile, odd):
    nk = tile.shape[0]
    z = jnp.zeros((nk, HEAD_DIM), tile.dtype)
    half = tile[:, HEAD_DIM:] if odd else tile[:, :HEAD_DIM]
    return jnp.concatenate([half, z], axis=1), jnp.concatenate([z, half], axis=1)


def _band_core(k, v, get_q, put_o, put_lse, sink_ref, a, *, slopes, n_kv, group, dil, inclusive,
               has_prev):
    has_sink = sink_ref is not None
    with_lse = put_lse is not None
    vt = v.T
    nk = k.shape[0]
    row = lax.broadcasted_iota(jnp.int32, (nk, BLK), 0)
    col = lax.broadcasted_iota(jnp.int32, (nk, BLK), 1)
    da = col - row + (nk - BLK)
    near = (da <= BLK) if inclusive else (da < BLK)
    if has_prev:
        exists = jnp.where(row >= BLK, 1, jnp.where(a > 0, 1, 0))
    else:
        exists = 1
    ok = jnp.where(da >= 0, jnp.where(near, exists, 0), 0)
    dm = jnp.where(ok > 0, (da * dil).astype(F32), BIG)
    pairs = group // 2
    logits = []
    for kv in range(n_kv):
        ka, kb = _split_heads_k(k[:, (kv // 2) * BLK:(kv // 2 + 1) * BLK], kv % 2 == 1)
        qst = jnp.concatenate([get_q(kv * pairs + p_) for p_ in range(pairs)], axis=0)
        logits.append((lax.dot_general(ka, qst, NT, preferred_element_type=F32),
                       lax.dot_general(kb, qst, NT, preferred_element_type=F32)))
    accs, inv_all = [], []
    for kv in range(n_kv):
        s_even, s_odd = logits[kv]
        pts, invs = [], []
        for g in range(group):
            p_, odd = divmod(g, 2)
            h = kv * group + g
            s = (s_odd if odd else s_even)[:, p_ * BLK:(p_ + 1) * BLK] - slopes[h] * dm
            m = jnp.max(s, axis=0, keepdims=True)
            if has_sink:
                sink = sink_ref[h] * LOG2E
                m = jnp.maximum(m, sink)
            p = jnp.exp2(s - m)
            den = jnp.sum(p, axis=0, keepdims=True)
            if has_sink:
                den = den + jnp.exp2(sink - m)
            if with_lse:
                put_lse(h, m + jnp.log2(den))
            pts.append(p.astype(BF))
            invs.append(1.0 / den)
        accs.append(jnp.dot(vt[kv * HEAD_DIM:(kv + 1) * HEAD_DIM, :], jnp.concatenate(pts, axis=1),
                            preferred_element_type=F32))
        inv_all.append(invs)
    for kv in range(n_kv):
        acc, invs = accs[kv], inv_all[kv]
        for p_ in range(pairs):
            oe = acc[:, (2 * p_) * BLK:(2 * p_ + 1) * BLK] * invs[2 * p_]
            oo = acc[:, (2 * p_ + 1) * BLK:(2 * p_ + 2) * BLK] * invs[2 * p_ + 1]
            put_o(kv * pairs + p_, jnp.concatenate([oe, oo], axis=0).T)


def _swa_kernel(sink_ref, q_ref, kvp_ref, kvc_ref, o_ref, *, slopes, n_kv, group):
    dkv = n_kv * HEAD_DIM
    k = jnp.concatenate([kvp_ref[:, 0:dkv], kvc_ref[:, 0:dkv]], axis=0)
    v = jnp.concatenate([kvp_ref[:, dkv:2 * dkv], kvc_ref[:, dkv:2 * dkv]], axis=0)

    def put_o(tq, blk):
        o_ref[:, tq * BLK:(tq + 1) * BLK] = blk.astype(o_ref.dtype)

    _band_core(k, v, lambda tq: q_ref[:, tq * BLK:(tq + 1) * BLK], put_o, None, sink_ref,
               pl.program_id(1), slopes=slopes, n_kv=n_kv, group=group, dil=1, inclusive=False,
               has_prev=True)


def _swa_attention(qkv, sinks, *, batch, seq, n_heads, n_kv):
    t = qkv.shape[0]
    nq = seq // BLK
    dq, dkv = n_heads * HEAD_DIM, n_kv * HEAD_DIM
    assert dq % (2 * dkv) == 0 and nq > 1
    kvb = dq // (2 * dkv)
    return pl.pallas_call(
        functools.partial(_swa_kernel, slopes=_alibi_slopes(n_heads), n_kv=n_kv, group=n_heads // n_kv),
        out_shape=jax.ShapeDtypeStruct((t, dq), BF),
        grid=(batch, nq),
        in_specs=[pl.BlockSpec(memory_space=pltpu.SMEM),
                  pl.BlockSpec((BLK, dq), lambda b, i: (b * nq + i, 0)),
                  pl.BlockSpec((BLK, 2 * dkv), lambda b, i: (b * nq + jnp.maximum(i - 1, 0), kvb)),
                  pl.BlockSpec((BLK, 2 * dkv), lambda b, i: (b * nq + i, kvb))],
        out_specs=pl.BlockSpec((BLK, dq), lambda b, i: (b * nq + i, 0)),
        compiler_params=_cparams(("parallel", "parallel")),
        name="swa_attn",
    )(sinks.astype(F32), qkv, qkv, qkv)


def _dil_kernel(*refs, slopes, n_kv, group, dil, has_prev):
    refs = list(refs)
    q_refs = [refs.pop(0), refs.pop(0)]
    kv_refs = [refs.pop(0) for _ in range(2 if has_prev else 1)]
    o_ref, lse_ref, lse_scr = refs
    r = pl.program_id(2)
    rows = pl.ds(r, BLK, stride=dil) if dil > 1 else slice(None)
    ktiles = n_kv * HEAD_DIM // BLK

    def gather(ref, lo, hi):
        return jnp.concatenate([ref[cc, rows, :] for cc in range(lo, hi)], axis=1).astype(BF)

    k = jnp.concatenate([gather(ref, 0, ktiles) for ref in kv_refs], axis=0)
    v = jnp.concatenate([gather(ref, ktiles, 2 * ktiles) for ref in kv_refs], axis=0)
    per_ref = q_refs[0].shape[0]
    lse_scr[...] = jnp.zeros(lse_scr.shape, F32)

    def put_o(tq, blk):
        o_ref[tq, rows, :] = blk

    def put_lse(h, row):
        lse_scr[h:h + 1, :] = row

    _band_core(k, v, lambda tq: q_refs[tq // per_ref][tq % per_ref, rows, :].astype(BF), put_o, put_lse,
               None, pl.program_id(1), slopes=slopes, n_kv=n_kv, group=group, dil=dil, inclusive=True,
               has_prev=has_prev)
    lse_ref[rows, :] = lse_scr[...].T


def _dil_attention(proj_t, g, *, batch, seq, n_heads, n_kv, dil):
    t = proj_t.shape[1]
    cls = seq // dil
    assert seq % dil == 0 and cls % BLK == 0 and n_heads * HEAD_DIM == 8 * BLK and 2 * n_kv * HEAD_DIM == 4 * BLK
    n_a = cls // BLK
    rws = BLK * dil
    has_prev = n_a > 1
    cur = lambda blk: pl.BlockSpec((4, rws, BLK), lambda b, a, r: (blk, b * n_a + a, 0))
    prev = lambda blk: pl.BlockSpec((4, rws, BLK), lambda b, a, r: (blk, b * n_a + jnp.maximum(a - 1, 0), 0))
    specs = [cur(3 * g), cur(3 * g + 1)] + ([prev(3 * g + 2)] if has_prev else []) + [cur(3 * g + 2)]
    return pl.pallas_call(
        functools.partial(_dil_kernel, slopes=_alibi_slopes(n_heads), n_kv=n_kv, group=n_heads // n_kv,
                          dil=dil, has_prev=has_prev),
        out_shape=[jax.ShapeDtypeStruct((8, t, BLK), F32), jax.ShapeDtypeStruct((t, BLK), F32)],
        grid=(batch, n_a, dil),
        in_specs=specs,
        out_specs=[pl.BlockSpec((8, rws, BLK), lambda b, a, r: (0, b * n_a + a, 0)),
                   pl.BlockSpec((rws, BLK), lambda b, a, r: (b * n_a + a, 0))],
        scratch_shapes=[pltpu.VMEM((BLK, BLK), F32)],
        compiler_params=_cparams(("parallel", "parallel", "arbitrary")),
        name=f"dil_attn{g}",
    )(*([proj_t] * len(specs)))


def _dsa_kernel(q_ref, qi_ref, wq_ref, kv_ref, ki_ref, o_ref,
                vt_scr, ka_scr, kb_scr, qst_scr, qist_scr, sc_scr, m_scr, l_scr, acc_scr,
                *, slopes, topk, ch, n_heads, n_idx, hd_dim, idx_dim):
    i = pl.program_id(1)
    t0 = i * BLK
    nch = (t0 + BLK - 1) // ch + 1

    @pl.when(i == 0)
    def _():
        vt_scr[...] = kv_ref[:, hd_dim:2 * hd_dim].T
        ka, kb = _split_heads_k(ki_ref[...], False)
        ka_scr[...] = ka
        kb_scr[...] = kb

    for h in range(n_heads):
        qst_scr[h * BLK:(h + 1) * BLK, :] = q_ref[:, h * hd_dim:(h + 1) * hd_dim]
    for p in range(n_idx // 2):
        qist_scr[p * BLK:(p + 1) * BLK, :] = qi_ref[:, p * BLK:(p + 1) * BLK]

    wscale = float(idx_dim ** -0.5) * float(n_idx ** -0.5)
    wt = wq_ref[...].astype(F32).T * wscale
    row = lax.broadcasted_iota(jnp.int32, (ch, BLK), 0)
    col = lax.broadcasted_iota(jnp.int32, (ch, BLK), 1)
    cr = col - row

    def p1(cc, carry):
        start = pl.multiple_of(cc * ch, ch)
        qi = qist_scr[...]
        rel_e = lax.dot_general(ka_scr[pl.ds(start, ch), :], qi, NT, preferred_element_type=F32)
        rel_o = lax.dot_general(kb_scr[pl.ds(start, ch), :], qi, NT, preferred_element_type=F32)
        score = jnp.zeros((ch, BLK), F32)
        for p in range(n_idx // 2):
            we = wt[idx_dim + 2 * p:idx_dim + 2 * p + 1, :]
            wo = wt[idx_dim + 2 * p + 1:idx_dim + 2 * p + 2, :]
            score = score + we * jnp.maximum(rel_e[:, p * BLK:(p + 1) * BLK], 0.0)
            score = score + wo * jnp.maximum(rel_o[:, p * BLK:(p + 1) * BLK], 0.0)
        sc_scr[pl.ds(start, ch), :] = jnp.where(cr + (t0 - start) >= 0, score, -jnp.inf)
        return carry

    lax.fori_loop(0, nch, p1, 0)

    tq = t0 + lax.broadcasted_iota(jnp.int32, (1, BLK), 1)
    kq = jnp.minimum(tq + 1, topk).astype(F32)

    def key_to_f32(key):
        bits = jnp.where(key < 0, key ^ jnp.int32(0x7FFFFFFF), key)
        return lax.bitcast_convert_type(bits, F32)

    def count_ge(cf):
        lanes = 64

        def body(cc, acc):
            blk = sc_scr[pl.ds(pl.multiple_of(cc * ch, ch), ch), :]
            hit = jnp.where(blk >= cf, 1.0, 0.0)
            for j in range(ch // lanes):
                acc = acc + hit[j * lanes:(j + 1) * lanes, :]
            return acc

        acc = lax.fori_loop(0, nch, body, jnp.zeros((lanes, BLK), F32))
        return jnp.sum(acc, axis=0, keepdims=True)

    ans0 = jnp.where(count_ge(jnp.zeros((1, BLK), F32)) >= kq,
                     jnp.int32(0), jnp.int32(-2147483648))

    def radix(b, ans):
        cand = ans | lax.shift_left(jnp.int32(1), 30 - b)
        return jnp.where(count_ge(key_to_f32(cand)) >= kq, cand, ans)

    thr = key_to_f32(lax.fori_loop(0, 31, radix, ans0))

    def p3(cc, carry):
        start = pl.multiple_of(cc * ch, ch)
        sel = sc_scr[pl.ds(start, ch), :] >= thr
        sc_scr[pl.ds(start, ch), :] = jnp.where(sel, (cr + (t0 - start)).astype(F32), BIG)
        return carry

    lax.fori_loop(0, nch, p3, 0)

    m_scr[...] = jnp.full(m_scr.shape, NEG, F32)
    l_scr[...] = jnp.zeros(l_scr.shape, F32)
    acc_scr[...] = jnp.zeros(acc_scr.shape, F32)

    def p4(cc, carry):
        start = pl.multiple_of(cc * ch, ch)
        s_all = lax.dot_general(kv_ref[pl.ds(start, ch), 0:hd_dim], qst_scr[...], NT,
                                preferred_element_type=F32)
        dm = sc_scr[pl.ds(start, ch), :]
        pts, alphas = [], []
        for h in range(n_heads):
            hs = slice(h * BLK, (h + 1) * BLK)
            s = s_all[:, hs] - slopes[h] * dm
            m_old = m_scr[:, hs]
            m_new = jnp.maximum(m_old, jnp.max(s, axis=0, keepdims=True))
            alpha = jnp.exp2(m_old - m_new)
            p = jnp.exp2(s - m_new)
            l_scr[:, hs] = alpha * l_scr[:, hs] + jnp.sum(p, axis=0, keepdims=True)
            m_scr[:, hs] = m_new
            pts.append(p.astype(BF))
            alphas.append(alpha)
        pv = jnp.dot(vt_scr[:, pl.ds(start, ch)], jnp.concatenate(pts, axis=1),
                     preferred_element_type=F32)
        acc_scr[...] = jnp.concatenate(alphas, axis=1) * acc_scr[...] + pv
        return carry

    lax.fori_loop(0, nch, p4, 0)
    for h in range(n_heads):
        hs = slice(h * BLK, (h + 1) * BLK)
        o_ref[:, h * hd_dim:(h + 1) * hd_dim] = (acc_scr[:, hs] * (1.0 / l_scr[:, hs])).T.astype(o_ref.dtype)


def _dsa_attention(qq, proj, kv, *, batch, seq, n_heads, hd_dim, n_idx, idx_dim, kidx_block):
    t = qq.shape[0]
    nq = seq // BLK
    dq = n_heads * hd_dim
    di = n_idx * idx_dim
    assert dq % di == 0 and hd_dim == BLK and idx_dim == HEAD_DIM
    topk = min(TOPK_MAX, seq // 4)
    ch = min(512, seq)
    return pl.pallas_call(
        functools.partial(_dsa_kernel, slopes=_alibi_slopes(n_heads), topk=topk, ch=ch,
                          n_heads=n_heads, n_idx=n_idx, hd_dim=hd_dim, idx_dim=idx_dim),
        out_shape=jax.ShapeDtypeStruct((t, dq), BF),
        grid=(batch, nq),
        in_specs=[pl.BlockSpec((BLK, dq), lambda b, i: (b * nq + i, 0)),
                  pl.BlockSpec((BLK, di), lambda b, i: (b * nq + i, dq // di)),
                  pl.BlockSpec((BLK, BLK), lambda b, i: (b * nq + i, kidx_block)),
                  pl.BlockSpec((seq, 2 * hd_dim), lambda b, i: (b, 0)),
                  pl.BlockSpec((seq, BLK), lambda b, i: (b, kidx_block))],
        out_specs=pl.BlockSpec((BLK, dq), lambda b, i: (b * nq + i, 0)),
        scratch_shapes=[pltpu.VMEM((hd_dim, seq), BF),
                        pltpu.VMEM((seq, BLK), BF),
                        pltpu.VMEM((seq, BLK), BF),
                        pltpu.VMEM((n_heads * BLK, hd_dim), BF),
                        pltpu.VMEM((n_idx // 2 * BLK, BLK), BF),
                        pltpu.VMEM((seq, BLK), F32),
                        pltpu.VMEM((1, n_heads * BLK), F32),
                        pltpu.VMEM((1, n_heads * BLK), F32),
                        pltpu.VMEM((hd_dim, n_heads * BLK), F32)],
        compiler_params=_cparams(("parallel", "arbitrary")),
        name="dsa_attn",
    )(qq, qq, proj, kv, proj)


def _mixer_swa(xf, pro, gate, w_in, w_out, j, qn_g, kn_g, sinks, *, batch, seq):
    d = xf.shape[1]
    n_heads = d // HEAD_DIM
    n_kv = n_heads // 8
    dq, dkv = n_heads * HEAD_DIM, n_kv * HEAD_DIM
    gs = np.concatenate([np.full(dq + dkv, HEAD_DIM), np.zeros(dkv)]).astype(np.int64)
    gain = jnp.concatenate([jnp.tile(qn_g * (HEAD_DIM ** -0.5 * LOG2E), n_heads), jnp.tile(kn_g, n_kv),
                            jnp.ones((dkv,), F32)])
    qkv = _matmul(xf, w_in, j, seq=seq, tm=1024, tn=512, out_dtype=BF, prologue=pro,
                  gnorm=(gs, gain), name="swa_in")
    o = _swa_attention(qkv, sinks, batch=batch, seq=seq, n_heads=n_heads, n_kv=n_kv)
    return _out_proj(o, w_out, j, xf, gate, seq=seq, tm=512, name="swa_out")


def _mixer_dsa(xf, pro, gate, w_in, qlat_g, kvlat_g, w_uq, w_ukv, w_idx_q, qn_g, kn_g, w_out, j,
               *, batch, seq):
    qrank, kvrank = qlat_g.shape[0], kvlat_g.shape[0]
    hd_dim = kn_g.shape[0]
    n_heads = w_uq.shape[1] // hd_dim
    n_in = w_in.shape[1]
    n_idx = 16
    idx_dim = n_in - qrank - kvrank - n_idx
    assert idx_dim + n_idx <= 128 and (qrank + kvrank) % 128 == 0
    pad = qrank + kvrank + 128 - n_in
    w_in_p = jnp.pad(w_in, ((0, 0), (0, pad))).astype(BF)[None]
    gs = np.concatenate([np.full(qrank, qrank), np.full(kvrank, kvrank), np.zeros(128)]).astype(np.int64)
    gain = jnp.concatenate([qlat_g, kvlat_g, jnp.ones((128,), F32)])
    proj = _matmul(xf, w_in_p, 0, seq=seq, tm=512, tn=qrank + kvrank + 128, out_dtype=BF, prologue=pro,
                   gnorm=(gs, gain), name="dsa_in")
    w_q = jnp.concatenate([w_uq, w_idx_q], axis=1).astype(BF)[None]
    nq_cols, ni_cols = w_uq.shape[1], w_idx_q.shape[1]
    gs_q = np.concatenate([np.full(nq_cols, hd_dim), np.zeros(ni_cols)]).astype(np.int64)
    gain_q = jnp.concatenate([jnp.tile(qn_g * (hd_dim ** -0.5 * LOG2E), n_heads), jnp.ones((ni_cols,), F32)])
    qq = _matmul(proj, w_q, 0, seq=seq, tm=1024, tn=256, out_dtype=BF, lhs_cols=(0, qrank),
                 gnorm=(gs_q, gain_q), name="dsa_q")
    gs_kv = np.concatenate([np.full(hd_dim, hd_dim), np.zeros(hd_dim)]).astype(np.int64)
    gain_kv = jnp.concatenate([kn_g, jnp.ones((hd_dim,), F32)])
    kv = _matmul(proj, w_ukv[None], 0, seq=seq, tm=1024, tn=2 * hd_dim, out_dtype=BF,
                 lhs_cols=(qrank // kvrank, kvrank), gnorm=(gs_kv, gain_kv), name="dsa_kv")
    o = _dsa_attention(qq, proj, kv, batch=batch, seq=seq, n_heads=n_heads, hd_dim=hd_dim,
                       n_idx=n_idx, idx_dim=idx_dim, kidx_block=(qrank + kvrank) // 128)
    return _out_proj(o, w_out, j, xf, gate, seq=seq, tm=512, name="dsa_out")


def _mixer_dil(xf, pro, gate, w_in, w_out, j, qn_g, kn_g, *, batch, seq):
    n_groups = len(DIL_PATTERNS)
    n_heads = w_out.shape[1] // HEAD_DIM
    n_kv = n_heads // 4
    dq, dkv = n_heads * HEAD_DIM, n_kv * HEAD_DIM
    gcols = dq + 2 * dkv
    gs = np.tile(np.concatenate([np.full(dq + dkv, HEAD_DIM), np.zeros(dkv)]), n_groups).astype(np.int64)
    gain = jnp.concatenate([
        jnp.concatenate([jnp.tile(qn_g[g] * (HEAD_DIM ** -0.5 * LOG2E), n_heads), jnp.tile(kn_g[g], n_kv),
                         jnp.ones((dkv,), F32)]) for g in range(n_groups)])
    assert gcols == 12 * BLK
    proj_t = _matmul(xf, w_in, j, seq=seq, tm=1024, tn=512, out_dtype=F32, prologue=pro,
                     gnorm=(gs, gain), tile_major=True, name="dil_in")
    outs, lses = [], []
    for g, (win, dil) in enumerate(DIL_PATTERNS):
        assert win == BLK * dil
        o, lse = _dil_attention(proj_t, g, batch=batch, seq=seq, n_heads=n_heads, n_kv=n_kv, dil=dil)
        outs.append(o)
        lses.append(lse)
    return _out_proj(None, w_out, j, xf, gate, seq=seq, tm=512, mix=(outs, lses), name="dil_out")


def _mlp(xf, pro, gate, w1, w2, i, *, seq):
    u = _matmul(xf, w1, i, seq=seq, tm=1024, tn=512, out_dtype=BF, prologue=pro,
                relu2=True, name="mlp_up")
    return _matmul_ktiled_resid(u, w2, i, xf, gate, seq=seq, tm=1024, tn=1024, tk=2048,
                                name="mlp_down")


def kernel(x, c, ada_w, ada_b, norm1_g, norm2_g, mlp_w1, mlp_w2, swa_w_in, swa_qn_g, swa_kn_g, swa_sinks, swa_w_out, dsa_w_in, dsa_qlat_g, dsa_kvlat_g, dsa_w_uq, dsa_w_ukv, dsa_w_idx_q, dsa_qn_g, dsa_kn_g, dsa_w_out, dil_w_in, dil_qn_g, dil_kn_g, dil_w_out):
    batch, seq, d = x.shape
    depth = ada_w.shape[0]
    mod = _adaln_mod(c, ada_w, ada_b)
    xf = x.reshape(batch * seq, d)
    mlp_w2_bf = mlp_w2.astype(BF)
    swa_w_in_bf, dil_w_in_bf = swa_w_in.astype(BF), dil_w_in.astype(BF)
    swa_w_out_bf, dsa_w_out_bf, dil_w_out_bf = (w.astype(BF) for w in (swa_w_out, dsa_w_out, dil_w_out))
    for i in range(depth):
        m = mod[i].reshape(batch, 1, 6 * d)
        sh1, sc1, g1, sh2, sc2, g2 = [m[:, :, q * d:(q + 1) * d] for q in range(6)]
        pro = (norm1_g[i], sh1, sc1)
        kind, j = i % 3, i // 3
        if kind == 0:
            xf = _mixer_swa(xf, pro, g1, swa_w_in_bf, swa_w_out_bf, j, swa_qn_g[j], swa_kn_g[j], swa_sinks[j],
                            batch=batch, seq=seq)
        elif kind == 1:
            xf = _mixer_dsa(xf, pro, g1, dsa_w_in[j], dsa_qlat_g[j], dsa_kvlat_g[j], dsa_w_uq[j],
                            dsa_w_ukv[j], dsa_w_idx_q[j], dsa_qn_g[j], dsa_kn_g[j], dsa_w_out_bf, j,
                            batch=batch, seq=seq)
        else:
            xf = _mixer_dil(xf, pro, g1, dil_w_in_bf, dil_w_out_bf, j, dil_qn_g[j], dil_kn_g[j],
                            batch=batch, seq=seq)
        xf = _mlp(xf, (norm2_g[i], sh2, sc2), g2, mlp_w1, mlp_w2_bf, i, seq=seq)
    return xf.reshape(batch, seq, d)
```

```python
import functools

import jax
import jax.numpy as jnp
import numpy as np
from jax import lax
from jax.experimental import pallas as pl
from jax.experimental.pallas import tpu as pltpu

F32 = jnp.float32
BF = jnp.bfloat16
NT = (((1,), (1,)), ((), ()))

EPS = 1e-6
NEG = -1e30
BIG = 1e30
LOG2E = 1.4426950408889634
BLK = 128
HEAD_DIM = 64
TOPK_MAX = 256
DIL_PATTERNS = ((128, 1), (512, 4), (2048, 16))
VMEM_LIMIT = 56 * 1024 * 1024


def _alibi_slopes(n):
    return [float(2.0 ** (-8.0 * (i + 1) / n)) * LOG2E for i in range(n)]


def _cparams(sem):
    return pltpu.CompilerParams(dimension_semantics=sem, vmem_limit_bytes=VMEM_LIMIT)


def _mod_kernel(c_ref, w_ref, b_ref, o_ref):
    c = c_ref[...]
    cond = c * (1.0 / (1.0 + jnp.exp(-c)))
    o_ref[0] = jnp.dot(cond.astype(BF), w_ref[0].astype(BF),
                       preferred_element_type=F32) + b_ref[0]


def _adaln_mod(c, ada_w, ada_b):
    depth, d, n = ada_w.shape
    b = c.shape[0]
    tn = 1024
    return pl.pallas_call(
        _mod_kernel,
        out_shape=jax.ShapeDtypeStruct((depth, b, n), F32),
        grid=(depth, n // tn),
        in_specs=[pl.BlockSpec((b, d), lambda l, j: (0, 0)),
                  pl.BlockSpec((1, d, tn), lambda l, j: (l, 0, j)),
                  pl.BlockSpec((1, 1, tn), lambda l, j: (l, 0, j))],
        out_specs=pl.BlockSpec((1, b, tn), lambda l, j: (l, 0, j)),
        compiler_params=_cparams(("parallel", "parallel")),
        name="adaln_mod",
    )(c, ada_w, ada_b.reshape(depth, 1, n))


def _mm_kernel(*refs, prologue, epilogue, tm, tile_major):
    refs = list(refs)
    lhs_ref = refs.pop(0)
    if prologue:
        ng_ref, sh_ref, sc_ref = refs.pop(0), refs.pop(0), refs.pop(0)
    w_ref = refs.pop(0)
    if epilogue == "gnorm":
        gm_ref, gain_ref, flag_ref = refs.pop(0), refs.pop(0), refs.pop(0)
    o_ref = refs.pop(0)

    if prologue:
        h_scr = refs.pop(0)
        rows = min(tm, 256)

        @pl.when(pl.program_id(1) == 0)
        def _():
            gmul = ng_ref[...] * (1.0 + sc_ref[0])
            shift = sh_ref[0]

            def body(rc, carry):
                r0 = pl.multiple_of(rc * rows, rows)
                xb = lhs_ref[pl.ds(r0, rows), :]
                ms = jnp.mean(xb * xb, axis=-1, keepdims=True)
                h = xb * lax.rsqrt(ms + EPS) * gmul + shift
                h_scr[pl.ds(r0, rows), :] = h.astype(BF)
                return carry

            lax.fori_loop(0, tm // rows, body, 0)

        lhs = h_scr[...]
    else:
        lhs = lhs_ref[...]

    acc = jnp.dot(lhs, w_ref[0].astype(BF), preferred_element_type=F32)
    if epilogue == "gnorm":
        ms = jnp.dot((acc * acc).astype(BF), gm_ref[0], preferred_element_type=F32)
        scale = jnp.where(flag_ref[...] > 0.0, lax.rsqrt(ms + EPS) * gain_ref[...], 1.0)
        out = acc * scale
    elif epilogue == "relu2":
        r = jnp.maximum(acc, 0.0)
        out = r * r
    else:
        out = acc
    if tile_major:
        for cc in range(out.shape[1] // BLK):
            o_ref[cc] = out[:, cc * BLK:(cc + 1) * BLK].astype(o_ref.dtype)
    else:
        o_ref[...] = out.astype(o_ref.dtype)


def _group_mats(gs, tn):
    n = gs.shape[0]
    col = np.arange(n)
    gsafe = np.maximum(gs, 1)
    grp = col // gsafe
    r = np.arange(tn)
    out = np.zeros((n // tn, tn, tn), np.float32)
    for j in range(n // tn):
        cj = col[j * tn:(j + 1) * tn]
        same = (grp[cj][None, :] == ((r[:, None] + j * tn) // gsafe[cj][None, :]))
        val = np.where(gs[cj] > 0, 1.0 / gsafe[cj], 0.0)[None, :]
        out[j] = np.where(same, val, 0.0)
    return jnp.asarray(out, dtype=BF)


def _matmul(lhs, w, wl, *, seq, tm, tn, out_dtype, lhs_cols=None, prologue=None,
            gnorm=None, relu2=False, tile_major=False, name="mm"):
    t = lhs.shape[0]
    _, k, n = w.shape
    tm = min(tm, seq)
    cb = 0 if lhs_cols is None else lhs_cols[0]
    nb = seq // tm
    args, specs = [lhs], [pl.BlockSpec((tm, k), lambda i, j: (i, cb))]
    if prologue is not None:
        ng, sh, sc = prologue
        args += [ng.reshape(1, k), sh, sc]
        specs += [pl.BlockSpec((1, k), lambda i, j: (0, 0)),
                  pl.BlockSpec((1, 1, k), lambda i, j: (i // nb, 0, 0)),
                  pl.BlockSpec((1, 1, k), lambda i, j: (i // nb, 0, 0))]
    args.append(w)
    specs.append(pl.BlockSpec((1, k, tn), lambda i, j: (wl, 0, j)))
    epilogue = "none"
    if gnorm is not None:
        epilogue = "gnorm"
        gs, gain = gnorm
        flag = jnp.asarray((gs > 0).astype(np.float32)).reshape(1, n)
        args += [_group_mats(gs, tn), gain.reshape(1, n).astype(F32), flag]
        specs += [pl.BlockSpec((1, tn, tn), lambda i, j: (j, 0, 0)),
                  pl.BlockSpec((1, tn), lambda i, j: (0, j)),
                  pl.BlockSpec((1, tn), lambda i, j: (0, j))]
    if relu2:
        epilogue = "relu2"
    scratch = [pltpu.VMEM((tm, k), BF)] if prologue is not None else []
    if tile_major:
        out_shape = jax.ShapeDtypeStruct((n // BLK, t, BLK), out_dtype)
        out_spec = pl.BlockSpec((tn // BLK, tm, BLK), lambda i, j: (j, i, 0))
    else:
        out_shape = jax.ShapeDtypeStruct((t, n), out_dtype)
        out_spec = pl.BlockSpec((tm, tn), lambda i, j: (i, j))
    return pl.pallas_call(
        functools.partial(_mm_kernel, prologue=prologue is not None, epilogue=epilogue, tm=tm,
                          tile_major=tile_major),
        out_shape=out_shape,
        grid=(t // tm, n // tn),
        in_specs=specs,
        out_specs=out_spec,
        scratch_shapes=scratch,
        compiler_params=_cparams(("parallel", "arbitrary")),
        name=name,
    )(*args)


def _out_kernel(*refs, n_mix):
    refs = list(refs)
    if n_mix:
        o_refs = [refs.pop(0) for _ in range(n_mix)]
        l_refs = [refs.pop(0) for _ in range(n_mix)]
        e_ref = refs.pop(0)
    else:
        lhs_ref = refs.pop(0)
    w_ref, res_ref, gate_ref, out_ref = refs
    if n_mix:
        ls = [r[...] for r in l_refs]
        mx = functools.reduce(jnp.maximum, ls)
        es = [jnp.exp2(l - mx) for l in ls]
        inv = 1.0 / functools.reduce(lambda a, b: a + b, es)
        e = e_ref[...]
        lhs = None
        for o_ref, eg in zip(o_refs, es):
            wg = eg * inv
            hi = wg.astype(BF)
            lo = (wg - hi.astype(F32)).astype(BF)
            wfull = (jnp.dot(hi, e, preferred_element_type=F32)
                     + jnp.dot(lo, e, preferred_element_type=F32))
            og = jnp.concatenate([o_ref[cc] for cc in range(o_ref.shape[0])], axis=1)
            term = wfull * og
            lhs = term if lhs is None else lhs + term
        lhs = lhs.astype(BF)
    else:
        lhs = lhs_ref[...]
    acc = jnp.dot(lhs, w_ref[0].astype(BF), preferred_element_type=F32)
    out_ref[...] = res_ref[...] + gate_ref[0] * acc


def _out_proj(lhs, w, wl, res, gate, *, seq, tm, mix=None, name):
    _, k, n = w.shape
    t = res.shape[0]
    tm = min(tm, seq)
    nb = seq // tm
    row = lambda width: pl.BlockSpec((tm, width), lambda i: (i, 0))
    if mix is not None:
        outs, lses = mix
        heads = k // HEAD_DIM
        e = np.zeros((BLK, k), np.float32)
        e[np.arange(k) // HEAD_DIM, np.arange(k)] = 1.0
        assert heads <= BLK
        args = list(outs) + list(lses) + [jnp.asarray(e, dtype=BF)]
        tiles = pl.BlockSpec((k // BLK, tm, BLK), lambda i: (0, i, 0))
        specs = [tiles] * len(outs) + [row(BLK)] * len(lses) + [pl.BlockSpec((BLK, k), lambda i: (0, 0))]
        n_mix = len(outs)
    else:
        args, specs, n_mix = [lhs], [row(k)], 0
    args += [w, res, gate]
    specs += [pl.BlockSpec((1, k, n), lambda i: (wl, 0, 0)),
              row(n),
              pl.BlockSpec((1, 1, n), lambda i: (i // nb, 0, 0))]
    return pl.pallas_call(
        functools.partial(_out_kernel, n_mix=n_mix),
        out_shape=jax.ShapeDtypeStruct((t, n), F32),
        grid=(t // tm,),
        in_specs=specs,
        out_specs=row(n),
        compiler_params=_cparams(("parallel",)),
        name=name,
    )(*args)


def _mmk_kernel(a_ref, w_ref, res_ref, gate_ref, o_ref, acc_ref):
    kk = pl.program_id(2)

    @pl.when(kk == 0)
    def _():
        acc_ref[...] = jnp.zeros_like(acc_ref)

    acc_ref[...] += jnp.dot(a_ref[...], w_ref[0], preferred_element_type=F32)

    @pl.when(kk == pl.num_programs(2) - 1)
    def _():
        o_ref[...] = res_ref[...] + gate_ref[0] * acc_ref[...]


def _matmul_ktiled_resid(a, w, wl, res, gate, *, seq, tm, tn, tk, name):
    t, k = a.shape
    n = w.shape[2]
    tm = min(tm, seq)
    nb = seq // tm
    return pl.pallas_call(
        _mmk_kernel,
        out_shape=jax.ShapeDtypeStruct((t, n), F32),
        grid=(t // tm, n // tn, k // tk),
        in_specs=[pl.BlockSpec((tm, tk), lambda i, j, q: (i, q)),
                  pl.BlockSpec((1, tk, tn), lambda i, j, q: (wl, q, j)),
                  pl.BlockSpec((tm, tn), lambda i, j, q: (i, j)),
                  pl.BlockSpec((1, 1, tn), lambda i, j, q: (i // nb, 0, j))],
        out_specs=pl.BlockSpec((tm, tn), lambda i, j, q: (i, j)),
        scratch_shapes=[pltpu.VMEM((tm, tn), F32)],
        compiler_params=_cparams(("parallel", "parallel", "arbitrary")),
        name=name,
    )(a, w, res, gate)


def _split_heads_k(tile, odd):
    nk = tile.shape[0]
    z = jnp.zeros((nk, HEAD_DIM), tile.dtype)
    half = tile[:, HEAD_DIM:] if odd else tile[:, :HEAD_DIM]
    return jnp.concatenate([half, z], axis=1), jnp.concatenate([z, half], axis=1)


def _band_core(k, v, get_q, put_o, put_lse, sink_ref, a, *, slopes, n_kv, group, dil, inclusive,
               has_prev):
    has_sink = sink_ref is not None
    with_lse = put_lse is not None
    vt = v.T
    nk = k.shape[0]
    row = lax.broadcasted_iota(jnp.int32, (nk, BLK), 0)
    col = lax.broadcasted_iota(jnp.int32, (nk, BLK), 1)
    da = col - row + (nk - BLK)
    near = (da <= BLK) if inclusive else (da < BLK)
    if has_prev:
        exists = jnp.where(row >= BLK, 1, jnp.where(a > 0, 1, 0))
    else:
        exists = 1
    ok = jnp.where(da >= 0, jnp.where(near, exists, 0), 0)
    dm = jnp.where(ok > 0, (da * dil).astype(F32), BIG)
    pairs = group // 2
    logits = []
    for kv in range(n_kv):
        ka, kb = _split_heads_k(k[:, (kv // 2) * BLK:(kv // 2 + 1) * BLK], kv % 2 == 1)
        qst = jnp.concatenate([get_q(kv * pairs + p_) for p_ in range(pairs)], axis=0)
        logits.append((lax.dot_general(ka, qst, NT, preferred_element_type=F32),
                       lax.dot_general(kb, qst, NT, preferred_element_type=F32)))
    accs, inv_all = [], []
    for kv in range(n_kv):
        s_even, s_odd = logits[kv]
        pts, invs = [], []
        for g in range(group):
            p_, odd = divmod(g, 2)
            h = kv * group + g
            s = (s_odd if odd else s_even)[:, p_ * BLK:(p_ + 1) * BLK] - slopes[h] * dm
            m = jnp.max(s, axis=0, keepdims=True)
            if has_sink:
                sink = sink_ref[h] * LOG2E
                m = jnp.maximum(m, sink)
            p = jnp.exp2(s - m)
            den = jnp.sum(p, axis=0, keepdims=True)
            if has_sink:
                den = den + jnp.exp2(sink - m)
            if with_lse:
                put_lse(h, m + jnp.log2(den))
            pts.append(p.astype(BF))
            invs.append(1.0 / den)
        accs.append(jnp.dot(vt[kv * HEAD_DIM:(kv + 1) * HEAD_DIM, :], jnp.concatenate(pts, axis=1),
                            preferred_element_type=F32))
        inv_all.append(invs)
    for kv in range(n_kv):
        acc, invs = accs[kv], inv_all[kv]
        for p_ in range(pairs):
            oe = acc[:, (2 * p_) * BLK:(2 * p_ + 1) * BLK] * invs[2 * p_]
            oo = acc[:, (2 * p_ + 1) * BLK:(2 * p_ + 2) * BLK] * invs[2 * p_ + 1]
            put_o(kv * pairs + p_, jnp.concatenate([oe, oo], axis=0).T)


def _swa_kernel(sink_ref, q_ref, kvp_ref, kvc_ref, o_ref, *, slopes, n_kv, group):
    dkv = n_kv * HEAD_DIM
    k = jnp.concatenate([kvp_ref[:, 0:dkv], kvc_ref[:, 0:dkv]], axis=0)
    v = jnp.concatenate([kvp_ref[:, dkv:2 * dkv], kvc_ref[:, dkv:2 * dkv]], axis=0)

    def put_o(tq, blk):
        o_ref[:, tq * BLK:(tq + 1) * BLK] = blk.astype(o_ref.dtype)

    _band_core(k, v, lambda tq: q_ref[:, tq * BLK:(tq + 1) * BLK], put_o, None, sink_ref,
               pl.program_id(1), slopes=slopes, n_kv=n_kv, group=group, dil=1, inclusive=False,
               has_prev=True)


def _swa_attention(qkv, sinks, *, batch, seq, n_heads, n_kv):
    t = qkv.shape[0]
    nq = seq // BLK
    dq, dkv = n_heads * HEAD_DIM, n_kv * HEAD_DIM
    assert dq % (2 * dkv) == 0 and nq > 1
    kvb = dq // (2 * dkv)
    return pl.pallas_call(
        functools.partial(_swa_kernel, slopes=_alibi_slopes(n_heads), n_kv=n_kv, group=n_heads // n_kv),
        out_shape=jax.ShapeDtypeStruct((t, dq), BF),
        grid=(batch, nq),
        in_specs=[pl.BlockSpec(memory_space=pltpu.SMEM),
                  pl.BlockSpec((BLK, dq), lambda b, i: (b * nq + i, 0)),
                  pl.BlockSpec((BLK, 2 * dkv), lambda b, i: (b * nq + jnp.maximum(i - 1, 0), kvb)),
                  pl.BlockSpec((BLK, 2 * dkv), lambda b, i: (b * nq + i, kvb))],
        out_specs=pl.BlockSpec((BLK, dq), lambda b, i: (b * nq + i, 0)),
        compiler_params=_cparams(("parallel", "parallel")),
        name="swa_attn",
    )(sinks.astype(F32), qkv, qkv, qkv)


def _dil_kernel(*refs, slopes, n_kv, group, dil, has_prev):
    refs = list(refs)
    q_refs = [refs.pop(0), refs.pop(0)]
    kv_refs = [refs.pop(0) for _ in range(2 if has_prev else 1)]
    o_ref, lse_ref, lse_scr = refs
    r = pl.program_id(2)
    rows = pl.ds(r, BLK, stride=dil) if dil > 1 else slice(None)
    ktiles = n_kv * HEAD_DIM // BLK

    def gather(ref, lo, hi):
        return jnp.concatenate([ref[cc, rows, :] for cc in range(lo, hi)], axis=1).astype(BF)

    k = jnp.concatenate([gather(ref, 0, ktiles) for ref in kv_refs], axis=0)
    v = jnp.concatenate([gather(ref, ktiles, 2 * ktiles) for ref in kv_refs], axis=0)
    per_ref = q_refs[0].shape[0]
    lse_scr[...] = jnp.zeros(lse_scr.shape, F32)

    def put_o(tq, blk):
        o_ref[tq, rows, :] = blk

    def put_lse(h, row):
        lse_scr[h:h + 1, :] = row

    _band_core(k, v, lambda tq: q_refs[tq // per_ref][tq % per_ref, rows, :].astype(BF), put_o, put_lse,
               None, pl.program_id(1), slopes=slopes, n_kv=n_kv, group=group, dil=dil, inclusive=True,
               has_prev=has_prev)
    lse_ref[rows, :] = lse_scr[...].T


def _dil_attention(proj_t, g, *, batch, seq, n_heads, n_kv, dil):
    t = proj_t.shape[1]
    cls = seq // dil
    assert seq % dil == 0 and cls % BLK == 0 and n_heads * HEAD_DIM == 8 * BLK and 2 * n_kv * HEAD_DIM == 4 * BLK
    n_a = cls // BLK
    rws = BLK * dil
    has_prev = n_a > 1
    cur = lambda blk: pl.BlockSpec((4, rws, BLK), lambda b, a, r: (blk, b * n_a + a, 0))
    prev = lambda blk: pl.BlockSpec((4, rws, BLK), lambda b, a, r: (blk, b * n_a + jnp.maximum(a - 1, 0), 0))
    specs = [cur(3 * g), cur(3 * g + 1)] + ([prev(3 * g + 2)] if has_prev else []) + [cur(3 * g + 2)]
    return pl.pallas_call(
        functools.partial(_dil_kernel, slopes=_alibi_slopes(n_heads), n_kv=n_kv, group=n_heads // n_kv,
                          dil=dil, has_prev=has_prev),
        out_shape=[jax.ShapeDtypeStruct((8, t, BLK), F32), jax.ShapeDtypeStruct((t, BLK), F32)],
        grid=(batch, n_a, dil),
        in_specs=specs,
        out_specs=[pl.BlockSpec((8, rws, BLK), lambda b, a, r: (0, b * n_a + a, 0)),
                   pl.BlockSpec((rws, BLK), lambda b, a, r: (b * n_a + a, 0))],
        scratch_shapes=[pltpu.VMEM((BLK, BLK), F32)],
        compiler_params=_cparams(("parallel", "parallel", "arbitrary")),
        name=f"dil_attn{g}",
    )(*([proj_t] * len(specs)))


def _dsa_kernel(q_ref, qi_ref, wq_ref, kv_ref, ki_ref, o_ref,
                vt_scr, ka_scr, kb_scr, qst_scr, qist_scr, sc_scr, bound_scr, m_scr, l_scr, acc_scr,
                *, slopes, topk, ch, seq, n_heads, n_idx, hd_dim, idx_dim):
    i = pl.program_id(1)
    t0 = i * BLK
    nch = (t0 + BLK - 1) // ch + 1

    @pl.when(i == 0)
    def _():
        vt_scr[...] = kv_ref[:, hd_dim:2 * hd_dim].T
        ka, kb = _split_heads_k(ki_ref[...], False)
        ka_scr[...] = ka
        kb_scr[...] = kb

    for h in range(n_heads):
        qst_scr[h * BLK:(h + 1) * BLK, :] = q_ref[:, h * hd_dim:(h + 1) * hd_dim]
    for p in range(n_idx // 2):
        qist_scr[p * BLK:(p + 1) * BLK, :] = qi_ref[:, p * BLK:(p + 1) * BLK]

    wscale = float(idx_dim ** -0.5) * float(n_idx ** -0.5)
    wt = wq_ref[...].astype(F32).T * wscale
    row = lax.broadcasted_iota(jnp.int32, (ch, BLK), 0)
    col = lax.broadcasted_iota(jnp.int32, (ch, BLK), 1)
    cr = col - row

    def p1(cc, carry):
        start = pl.multiple_of(cc * ch, ch)
        qi = qist_scr[...]
        rel_e = lax.dot_general(ka_scr[pl.ds(start, ch), :], qi, NT, preferred_element_type=F32)
        rel_o = lax.dot_general(kb_scr[pl.ds(start, ch), :], qi, NT, preferred_element_type=F32)
        score = jnp.zeros((ch, BLK), F32)
        for p in range(n_idx // 2):
            we = wt[idx_dim + 2 * p:idx_dim + 2 * p + 1, :]
            wo = wt[idx_dim + 2 * p + 1:idx_dim + 2 * p + 2, :]
            score = score + we * jnp.maximum(rel_e[:, p * BLK:(p + 1) * BLK], 0.0)
            score = score + wo * jnp.maximum(rel_o[:, p * BLK:(p + 1) * BLK], 0.0)
        sc_scr[pl.ds(start, ch), :] = jnp.where(cr + (t0 - start) >= 0, score, -jnp.inf)
        return carry

    lax.fori_loop(0, nch, p1, 0)

    tq = t0 + lax.broadcasted_iota(jnp.int32, (1, BLK), 1)
    kq = jnp.minimum(tq + 1, topk).astype(F32)

    def key_to_f32(key):
        bits = jnp.where(key < 0, key ^ jnp.int32(0x7FFFFFFF), key)
        return lax.bitcast_convert_type(bits, F32)

    def count(hit_fn):
        lanes = 64

        def body(cc, acc):
            start = pl.multiple_of(cc * ch, ch)
            hit = hit_fn(sc_scr[pl.ds(start, ch), :], start)
            for j in range(ch // lanes):
                acc = acc + hit[j * lanes:(j + 1) * lanes, :]
            return acc

        acc = lax.fori_loop(0, nch, body, jnp.zeros((lanes, BLK), F32))
        return jnp.sum(acc, axis=0, keepdims=True)

    def count_ge(cf):
        return count(lambda blk, start: jnp.where(blk >= cf, 1.0, 0.0))

    ans0 = jnp.where(count_ge(jnp.zeros((1, BLK), F32)) >= kq,
                     jnp.int32(0), jnp.int32(-2147483648))

    def radix(b, ans):
        cand = ans | lax.shift_left(jnp.int32(1), 30 - b)
        return jnp.where(count_ge(key_to_f32(cand)) >= kq, cand, ans)

    thr = key_to_f32(lax.fori_loop(0, 31, radix, ans0))

    bound_scr[...] = jnp.full(bound_scr.shape, seq, jnp.int32)

    @pl.when(jnp.max(count_ge(thr) - kq) > 0.0)
    def _():
        need = kq - count(lambda blk, start: jnp.where(blk > thr, 1.0, 0.0))

        def ties_below(x):
            return count(lambda blk, start: jnp.where(blk == thr, jnp.where(row + start < x, 1.0, 0.0), 0.0))

        nbits = (seq - 1).bit_length()

        def bsearch(b, x):
            cand = x | lax.shift_left(jnp.int32(1), nbits - 1 - b)
            return jnp.where(ties_below(cand) < need, cand, x)

        bound_scr[...] = lax.fori_loop(0, nbits, bsearch, jnp.zeros((1, BLK), jnp.int32)) + 1

    def p3(cc, carry):
        start = pl.multiple_of(cc * ch, ch)
        blk = sc_scr[pl.ds(start, ch), :]
        tie_ok = jnp.where(row + start < bound_scr[...], 1, 0)
        keep = jnp.where(blk > thr, 1, jnp.where(blk == thr, tie_ok, 0))
        sc_scr[pl.ds(start, ch), :] = jnp.where(keep > 0, (cr + (t0 - start)).astype(F32), BIG)
        return carry

    lax.fori_loop(0, nch, p3, 0)

    m_scr[...] = jnp.full(m_scr.shape, NEG, F32)
    l_scr[...] = jnp.zeros(l_scr.shape, F32)
    acc_scr[...] = jnp.zeros(acc_scr.shape, F32)

    def p4(cc, carry):
        start = pl.multiple_of(cc * ch, ch)
        s_all = lax.dot_general(kv_ref[pl.ds(start, ch), 0:hd_dim], qst_scr[...], NT,
                                preferred_element_type=F32)
        dm = sc_scr[pl.ds(start, ch), :]
        pts, alphas = [], []
        for h in range(n_heads):
            hs = slice(h * BLK, (h + 1) * BLK)
            s = s_all[:, hs] - slopes[h] * dm
            m_old = m_scr[:, hs]
            m_new = jnp.maximum(m_old, jnp.max(s, axis=0, keepdims=True))
            alpha = jnp.exp2(m_old - m_new)
            p = jnp.exp2(s - m_new)
            l_scr[:, hs] = alpha * l_scr[:, hs] + jnp.sum(p, axis=0, keepdims=True)
            m_scr[:, hs] = m_new
            pts.append(p.astype(BF))
            alphas.append(alpha)
        pv = jnp.dot(vt_scr[:, pl.ds(start, ch)], jnp.concatenate(pts, axis=1),
                     preferred_element_type=F32)
        acc_scr[...] = jnp.concatenate(alphas, axis=1) * acc_scr[...] + pv
        return carry

    lax.fori_loop(0, nch, p4, 0)
    for h in range(n_heads):
        hs = slice(h * BLK, (h + 1) * BLK)
        o_ref[:, h * hd_dim:(h + 1) * hd_dim] = (acc_scr[:, hs] * (1.0 / l_scr[:, hs])).T.astype(o_ref.dtype)


def _dsa_attention(qq, proj, kv, *, batch, seq, n_heads, hd_dim, n_idx, idx_dim, kidx_block):
    t = qq.shape[0]
    nq = seq // BLK
    dq = n_heads * hd_dim
    di = n_idx * idx_dim
    assert dq % di == 0 and hd_dim == BLK and idx_dim == HEAD_DIM
    topk = min(TOPK_MAX, seq // 4)
    ch = min(512, seq)
    return pl.pallas_call(
        functools.partial(_dsa_kernel, slopes=_alibi_slopes(n_heads), topk=topk, ch=ch, seq=seq,
                          n_heads=n_heads, n_idx=n_idx, hd_dim=hd_dim, idx_dim=idx_dim),
        out_shape=jax.ShapeDtypeStruct((t, dq), BF),
        grid=(batch, nq),
        in_specs=[pl.BlockSpec((BLK, dq), lambda b, i: (b * nq + i, 0)),
                  pl.BlockSpec((BLK, di), lambda b, i: (b * nq + i, dq // di)),
                  pl.BlockSpec((BLK, BLK), lambda b, i: (b * nq + i, kidx_block)),
                  pl.BlockSpec((seq, 2 * hd_dim), lambda b, i: (b, 0)),
                  pl.BlockSpec((seq, BLK), lambda b, i: (b, kidx_block))],
        out_specs=pl.BlockSpec((BLK, dq), lambda b, i: (b * nq + i, 0)),
        scratch_shapes=[pltpu.VMEM((hd_dim, seq), BF),
                        pltpu.VMEM((seq, BLK), BF),
                        pltpu.VMEM((seq, BLK), BF),
                        pltpu.VMEM((n_heads * BLK, hd_dim), BF),
                        pltpu.VMEM((n_idx // 2 * BLK, BLK), BF),
                        pltpu.VMEM((seq, BLK), F32),
                        pltpu.VMEM((1, BLK), jnp.int32),
                        pltpu.VMEM((1, n_heads * BLK), F32),
                        pltpu.VMEM((1, n_heads * BLK), F32),
                        pltpu.VMEM((hd_dim, n_heads * BLK), F32)],
        compiler_params=_cparams(("parallel", "arbitrary")),
        name="dsa_attn",
    )(qq, qq, proj, kv, proj)


def _mixer_swa(xf, pro, gate, w_in, w_out, j, qn_g, kn_g, sinks, *, batch, seq):
    d = xf.shape[1]
    n_heads = d // HEAD_DIM
    n_kv = n_heads // 8
    dq, dkv = n_heads * HEAD_DIM, n_kv * HEAD_DIM
    gs = np.concatenate([np.full(dq + dkv, HEAD_DIM), np.zeros(dkv)]).astype(np.int64)
    gain = jnp.concatenate([jnp.tile(qn_g * (HEAD_DIM ** -0.5 * LOG2E), n_heads), jnp.tile(kn_g, n_kv),
                            jnp.ones((dkv,), F32)])
    qkv = _matmul(xf, w_in, j, seq=seq, tm=1024, tn=512, out_dtype=BF, prologue=pro,
                  gnorm=(gs, gain), name="swa_in")
    o = _swa_attention(qkv, sinks, batch=batch, seq=seq, n_heads=n_heads, n_kv=n_kv)
    return _out_proj(o, w_out, j, xf, gate, seq=seq, tm=512, name="swa_out")


def _mixer_dsa(xf, pro, gate, w_in, qlat_g, kvlat_g, w_uq, w_ukv, w_idx_q, qn_g, kn_g, w_out, j,
               *, batch, seq):
    qrank, kvrank = qlat_g.shape[0], kvlat_g.shape[0]
    hd_dim = kn_g.shape[0]
    n_heads = w_uq.shape[1] // hd_dim
    n_in = w_in.shape[1]
    n_idx = 16
    idx_dim = n_in - qrank - kvrank - n_idx
    assert idx_dim + n_idx <= 128 and (qrank + kvrank) % 128 == 0
    pad = qrank + kvrank + 128 - n_in
    w_in_p = jnp.pad(w_in, ((0, 0), (0, pad))).astype(BF)[None]
    gs = np.concatenate([np.full(qrank, qrank), np.full(kvrank, kvrank), np.zeros(128)]).astype(np.int64)
    gain = jnp.concatenate([qlat_g, kvlat_g, jnp.ones((128,), F32)])
    proj = _matmul(xf, w_in_p, 0, seq=seq, tm=512, tn=qrank + kvrank + 128, out_dtype=BF, prologue=pro,
                   gnorm=(gs, gain), name="dsa_in")
    w_q = jnp.concatenate([w_uq, w_idx_q], axis=1).astype(BF)[None]
    nq_cols, ni_cols = w_uq.shape[1], w_idx_q.shape[1]
    gs_q = np.concatenate([np.full(nq_cols, hd_dim), np.zeros(ni_cols)]).astype(np.int64)
    gain_q = jnp.concatenate([jnp.tile(qn_g * (hd_dim ** -0.5 * LOG2E), n_heads), jnp.ones((ni_cols,), F32)])
    qq = _matmul(proj, w_q, 0, seq=seq, tm=1024, tn=256, out_dtype=BF, lhs_cols=(0, qrank),
                 gnorm=(gs_q, gain_q), name="dsa_q")
    gs_kv = np.concatenate([np.full(hd_dim, hd_dim), np.zeros(hd_dim)]).astype(np.int64)
    gain_kv = jnp.concatenate([kn_g, jnp.ones((hd_dim,), F32)])
    kv = _matmul(proj, w_ukv[None], 0, seq=seq, tm=1024, tn=2 * hd_dim, out_dtype=BF,
                 lhs_cols=(qrank // kvrank, kvrank), gnorm=(gs_kv, gain_kv), name="dsa_kv")
    o = _dsa_attention(qq, proj, kv, batch=batch, seq=seq, n_heads=n_heads, hd_dim=hd_dim,
                       n_idx=n_idx, idx_dim=idx_dim, kidx_block=(qrank + kvrank) // 128)
    return _out_proj(o, w_out, j, xf, gate, seq=seq, tm=512, name="dsa_out")


def _mixer_dil(xf, pro, gate, w_in, w_out, j, qn_g, kn_g, *, batch, seq):
    n_groups = len(DIL_PATTERNS)
    n_heads = w_out.shape[1] // HEAD_DIM
    n_kv = n_heads // 4
    dq, dkv = n_heads * HEAD_DIM, n_kv * HEAD_DIM
    gcols = dq + 2 * dkv
    gs = np.tile(np.concatenate([np.full(dq + dkv, HEAD_DIM), np.zeros(dkv)]), n_groups).astype(np.int64)
    gain = jnp.concatenate([
        jnp.concatenate([jnp.tile(qn_g[g] * (HEAD_DIM ** -0.5 * LOG2E), n_heads), jnp.tile(kn_g[g], n_kv),
                         jnp.ones((dkv,), F32)]) for g in range(n_groups)])
    assert gcols == 12 * BLK
    proj_t = _matmul(xf, w_in, j, seq=seq, tm=1024, tn=512, out_dtype=F32, prologue=pro,
                     gnorm=(gs, gain), tile_major=True, name="dil_in")
    outs, lses = [], []
    for g, (win, dil) in enumerate(DIL_PATTERNS):
        assert win == BLK * dil
        o, lse = _dil_attention(proj_t, g, batch=batch, seq=seq, n_heads=n_heads, n_kv=n_kv, dil=dil)
        outs.append(o)
        lses.append(lse)
    return _out_proj(None, w_out, j, xf, gate, seq=seq, tm=512, mix=(outs, lses), name="dil_out")


def _mlp(xf, pro, gate, w1, w2, i, *, seq):
    u = _matmul(xf, w1, i, seq=seq, tm=2048, tn=256, out_dtype=BF, prologue=pro,
                relu2=True, name="mlp_up")
    return _matmul_ktiled_resid(u, w2, i, xf, gate, seq=seq, tm=1024, tn=1024, tk=2048,
                                name="mlp_down")


def kernel(x, c, ada_w, ada_b, norm1_g, norm2_g, mlp_w1, mlp_w2, swa_w_in, swa_qn_g, swa_kn_g, swa_sinks, swa_w_out, dsa_w_in, dsa_qlat_g, dsa_kvlat_g, dsa_w_uq, dsa_w_ukv, dsa_w_idx_q, dsa_qn_g, dsa_kn_g, dsa_w_out, dil_w_in, dil_qn_g, dil_kn_g, dil_w_out):
    batch, seq, d = x.shape
    depth = ada_w.shape[0]
    mod = _adaln_mod(c, ada_w, ada_b)
    xf = x.reshape(batch * seq, d)
    mlp_w2_bf = mlp_w2.astype(BF)
    swa_w_in_bf, dil_w_in_bf = swa_w_in.astype(BF), dil_w_in.astype(BF)
    swa_w_out_bf, dsa_w_out_bf, dil_w_out_bf = (w.astype(BF) for w in (swa_w_out, dsa_w_out, dil_w_out))
    for i in range(depth):
        m = mod[i].reshape(batch, 1, 6 * d)
        sh1, sc1, g1, sh2, sc2, g2 = [m[:, :, q * d:(q + 1) * d] for q in range(6)]
        pro = (norm1_g[i], sh1, sc1)
        kind, j = i % 3, i // 3
        if kind == 0:
            xf = _mixer_swa(xf, pro, g1, swa_w_in_bf, swa_w_out_bf, j, swa_qn_g[j], swa_kn_g[j], swa_sinks[j],
                            batch=batch, seq=seq)
        elif kind == 1:
            xf = _mixer_dsa(xf, pro, g1, dsa_w_in[j], dsa_qlat_g[j], dsa_kvlat_g[j], dsa_w_uq[j],
                            dsa_w_ukv[j], dsa_w_idx_q[j], dsa_qn_g[j], dsa_kn_g[j], dsa_w_out_bf, j,
                            batch=batch, seq=seq)
        else:
            xf = _mixer_dil(xf, pro, g1, dil_w_in_bf, dil_w_out_bf, j, dil_qn_g[j], dil_kn_g[j],
                            batch=batch, seq=seq)
        xf = _mlp(xf, (norm2_g[i], sh2, sc2), g2, mlp_w1, mlp_w2_bf, i, seq=seq)
    return xf.reshape(batch, seq, d)
```

```python
import functools

import jax
import jax.numpy as jnp
import numpy as np
from jax import lax
from jax.experimental import pallas as pl
from jax.experimental.pallas import tpu as pltpu

F32 = jnp.float32
BF = jnp.bfloat16
NT = (((1,), (1,)), ((), ()))

EPS = 1e-6
NEG = -1e30
BIG = 1e30
LOG2E = 1.4426950408889634
BLK = 128
HEAD_DIM = 64
TOPK_MAX = 256
DIL_PATTERNS = ((128, 1), (512, 4), (2048, 16))
VMEM_LIMIT = 56 * 1024 * 1024


def _alibi_slopes(n):
    return [float(2.0 ** (-8.0 * (i + 1) / n)) * LOG2E for i in range(n)]


def _cparams(sem):
    return pltpu.CompilerParams(dimension_semantics=sem, vmem_limit_bytes=VMEM_LIMIT)


def _mod_kernel(c_ref, w_ref, b_ref, o_ref):
    c = c_ref[...]
    cond = c * (1.0 / (1.0 + jnp.exp(-c)))
    o_ref[0] = jnp.dot(cond.astype(BF), w_ref[0].astype(BF),
                       preferred_element_type=F32) + b_ref[0]


def _adaln_mod(c, ada_w, ada_b):
    depth, d, n = ada_w.shape
    b = c.shape[0]
    tn = 1024
    return pl.pallas_call(
        _mod_kernel,
        out_shape=jax.ShapeDtypeStruct((depth, b, n), F32),
        grid=(depth, n // tn),
        in_specs=[pl.BlockSpec((b, d), lambda l, j: (0, 0)),
                  pl.BlockSpec((1, d, tn), lambda l, j: (l, 0, j)),
                  pl.BlockSpec((1, 1, tn), lambda l, j: (l, 0, j))],
        out_specs=pl.BlockSpec((1, b, tn), lambda l, j: (l, 0, j)),
        compiler_params=_cparams(("parallel", "parallel")),
        name="adaln_mod",
    )(c, ada_w, ada_b.reshape(depth, 1, n))


def _mm_kernel(*refs, prologue, epilogue, tm, tile_major):
    refs = list(refs)
    lhs_ref = refs.pop(0)
    if prologue:
        ng_ref, sh_ref, sc_ref = refs.pop(0), refs.pop(0), refs.pop(0)
    w_ref = refs.pop(0)
    if epilogue == "gnorm":
        gm_ref, gain_ref, flag_ref = refs.pop(0), refs.pop(0), refs.pop(0)
    o_ref = refs.pop(0)

    if prologue:
        h_scr = refs.pop(0)
        rows = min(tm, 256)

        @pl.when(pl.program_id(1) == 0)
        def _():
            gmul = ng_ref[...] * (1.0 + sc_ref[0])
            shift = sh_ref[0]

            def body(rc, carry):
                r0 = pl.multiple_of(rc * rows, rows)
                xb = lhs_ref[pl.ds(r0, rows), :]
                ms = jnp.mean(xb * xb, axis=-1, keepdims=True)
                h = xb * lax.rsqrt(ms + EPS) * gmul + shift
                h_scr[pl.ds(r0, rows), :] = h.astype(BF)
                return carry

            lax.fori_loop(0, tm // rows, body, 0)

        lhs = h_scr[...]
    else:
        lhs = lhs_ref[...]

    acc = jnp.dot(lhs, w_ref[0].astype(BF), preferred_element_type=F32)
    if epilogue == "gnorm":
        ms = jnp.dot((acc * acc).astype(BF), gm_ref[0], preferred_element_type=F32)
        scale = jnp.where(flag_ref[...] > 0.0, lax.rsqrt(ms + EPS) * gain_ref[...], 1.0)
        out = acc * scale
    elif epilogue == "relu2":
        r = jnp.maximum(acc, 0.0)
        out = r * r
    else:
        out = acc
    if tile_major:
        for cc in range(out.shape[1] // BLK):
            o_ref[cc] = out[:, cc * BLK:(cc + 1) * BLK].astype(o_ref.dtype)
    else:
        o_ref[...] = out.astype(o_ref.dtype)


def _group_mats(gs, tn):
    n = gs.shape[0]
    col = np.arange(n)
    gsafe = np.maximum(gs, 1)
    grp = col // gsafe
    r = np.arange(tn)
    out = np.zeros((n // tn, tn, tn), np.float32)
    for j in range(n // tn):
        cj = col[j * tn:(j + 1) * tn]
        same = (grp[cj][None, :] == ((r[:, None] + j * tn) // gsafe[cj][None, :]))
        val = np.where(gs[cj] > 0, 1.0 / gsafe[cj], 0.0)[None, :]
        out[j] = np.where(same, val, 0.0)
    return jnp.asarray(out, dtype=BF)


def _matmul(lhs, w, wl, *, seq, tm, tn, out_dtype, lhs_cols=None, prologue=None,
            gnorm=None, relu2=False, tile_major=False, name="mm"):
    t = lhs.shape[0]
    _, k, n = w.shape
    tm = min(tm, seq)
    cb = 0 if lhs_cols is None else lhs_cols[0]
    nb = seq // tm
    args, specs = [lhs], [pl.BlockSpec((tm, k), lambda i, j: (i, cb))]
    if prologue is not None:
        ng, sh, sc = prologue
        args += [ng.reshape(1, k), sh, sc]
        specs += [pl.BlockSpec((1, k), lambda i, j: (0, 0)),
                  pl.BlockSpec((1, 1, k), lambda i, j: (i // nb, 0, 0)),
                  pl.BlockSpec((1, 1, k), lambda i, j: (i // nb, 0, 0))]
    args.append(w)
    specs.append(pl.BlockSpec((1, k, tn), lambda i, j: (wl, 0, j)))
    epilogue = "none"
    if gnorm is not None:
        epilogue = "gnorm"
        gs, gain = gnorm
        flag = jnp.asarray((gs > 0).astype(np.float32)).reshape(1, n)
        args += [_group_mats(gs, tn), gain.reshape(1, n).astype(F32), flag]
        specs += [pl.BlockSpec((1, tn, tn), lambda i, j: (j, 0, 0)),
                  pl.BlockSpec((1, tn), lambda i, j: (0, j)),
                  pl.BlockSpec((1, tn), lambda i, j: (0, j))]
    if relu2:
        epilogue = "relu2"
    scratch = [pltpu.VMEM((tm, k), BF)] if prologue is not None else []
    if tile_major:
        out_shape = jax.ShapeDtypeStruct((n // BLK, t, BLK), out_dtype)
        out_spec = pl.BlockSpec((tn // BLK, tm, BLK), lambda i, j: (j, i, 0))
    else:
        out_shape = jax.ShapeDtypeStruct((t, n), out_dtype)
        out_spec = pl.BlockSpec((tm, tn), lambda i, j: (i, j))
    return pl.pallas_call(
        functools.partial(_mm_kernel, prologue=prologue is not None, epilogue=epilogue, tm=tm,
                          tile_major=tile_major),
        out_shape=out_shape,
        grid=(t // tm, n // tn),
        in_specs=specs,
        out_specs=out_spec,
        scratch_shapes=scratch,
        compiler_params=_cparams(("parallel", "arbitrary")),
        name=name,
    )(*args)


def _out_kernel(*refs, n_mix):
    refs = list(refs)
    if n_mix:
        o_refs = [refs.pop(0) for _ in range(n_mix)]
        l_refs = [refs.pop(0) for _ in range(n_mix)]
        e_ref = refs.pop(0)
    else:
        lhs_ref = refs.pop(0)
    w_ref, res_ref, gate_ref, ng_ref, sh_ref, sc_ref, out_ref, h_ref = refs
    if n_mix:
        ls = [r[...] for r in l_refs]
        mx = functools.reduce(jnp.maximum, ls)
        es = [jnp.exp2(l - mx) for l in ls]
        inv = 1.0 / functools.reduce(lambda a, b: a + b, es)
        e = e_ref[...]
        lhs = None
        for o_ref, eg in zip(o_refs, es):
            wg = eg * inv
            hi = wg.astype(BF)
            lo = (wg - hi.astype(F32)).astype(BF)
            wfull = (jnp.dot(hi, e, preferred_element_type=F32)
                     + jnp.dot(lo, e, preferred_element_type=F32))
            og = jnp.concatenate([o_ref[cc] for cc in range(o_ref.shape[0])], axis=1)
            term = wfull * og
            lhs = term if lhs is None else lhs + term
        lhs = lhs.astype(BF)
    else:
        lhs = lhs_ref[...]
    acc = jnp.dot(lhs, w_ref[0].astype(BF), preferred_element_type=F32)
    x_new = res_ref[...] + gate_ref[0] * acc
    out_ref[...] = x_new
    ms = jnp.mean(x_new * x_new, axis=-1, keepdims=True)
    h = x_new * lax.rsqrt(ms + EPS) * (ng_ref[...] * (1.0 + sc_ref[0])) + sh_ref[0]
    h_ref[...] = h.astype(h_ref.dtype)


def _out_proj(lhs, w, wl, res, gate, nxt, *, seq, tm, mix=None, name):
    _, k, n = w.shape
    t = res.shape[0]
    tm = min(tm, seq)
    nb = seq // tm
    row = lambda width: pl.BlockSpec((tm, width), lambda i: (i, 0))
    if mix is not None:
        outs, lses = mix
        heads = k // HEAD_DIM
        e = np.zeros((BLK, k), np.float32)
        e[np.arange(k) // HEAD_DIM, np.arange(k)] = 1.0
        assert heads <= BLK
        args = list(outs) + list(lses) + [jnp.asarray(e, dtype=BF)]
        tiles = pl.BlockSpec((k // BLK, tm, BLK), lambda i: (0, i, 0))
        specs = [tiles] * len(outs) + [row(BLK)] * len(lses) + [pl.BlockSpec((BLK, k), lambda i: (0, 0))]
        n_mix = len(outs)
    else:
        args, specs, n_mix = [lhs], [row(k)], 0
    ng, sh, sc = nxt
    per_batch = pl.BlockSpec((1, 1, n), lambda i: (i // nb, 0, 0))
    args += [w, res, gate, ng.reshape(1, n), sh, sc]
    specs += [pl.BlockSpec((1, k, n), lambda i: (wl, 0, 0)),
              row(n),
              per_batch,
              pl.BlockSpec((1, n), lambda i: (0, 0)),
              per_batch,
              per_batch]
    return pl.pallas_call(
        functools.partial(_out_kernel, n_mix=n_mix),
        out_shape=[jax.ShapeDtypeStruct((t, n), F32), jax.ShapeDtypeStruct((t, n), BF)],
        grid=(t // tm,),
        in_specs=specs,
        out_specs=[row(n), row(n)],
        compiler_params=_cparams(("parallel",)),
        name=name,
    )(*args)


def _mmk_kernel(a_ref, w_ref, res_ref, gate_ref, o_ref, acc_ref):
    kk = pl.program_id(2)

    @pl.when(kk == 0)
    def _():
        acc_ref[...] = jnp.zeros_like(acc_ref)

    acc_ref[...] += jnp.dot(a_ref[...], w_ref[0], preferred_element_type=F32)

    @pl.when(kk == pl.num_programs(2) - 1)
    def _():
        o_ref[...] = res_ref[...] + gate_ref[0] * acc_ref[...]


def _matmul_ktiled_resid(a, w, wl, res, gate, *, seq, tm, tn, tk, name):
    t, k = a.shape
    n = w.shape[2]
    tm = min(tm, seq)
    nb = seq // tm
    return pl.pallas_call(
        _mmk_kernel,
        out_shape=jax.ShapeDtypeStruct((t, n), F32),
        grid=(t // tm, n // tn, k // tk),
        in_specs=[pl.BlockSpec((tm, tk), lambda i, j, q: (i, q)),
                  pl.BlockSpec((1, tk, tn), lambda i, j, q: (wl, q, j)),
                  pl.BlockSpec((tm, tn), lambda i, j, q: (i, j)),
                  pl.BlockSpec((1, 1, tn), lambda i, j, q: (i // nb, 0, j))],
        out_specs=pl.BlockSpec((tm, tn), lambda i, j, q: (i, j)),
        scratch_shapes=[pltpu.VMEM((tm, tn), F32)],
        compiler_params=_cparams(("parallel", "parallel", "arbitrary")),
        name=name,
    )(a, w, res, gate)


def _split_heads_k(tile, odd):
    nk = tile.shape[0]
    z = jnp.zeros((nk, HEAD_DIM), tile.dtype)
    half = tile[:, HEAD_DIM:] if odd else tile[:, :HEAD_DIM]
    return jnp.concatenate([half, z], axis=1), jnp.concatenate([z, half], axis=1)


def _band_core(k, v, get_q, put_o, put_lse, sink_ref, a, *, slopes, n_kv, group, dil, inclusive,
               has_prev):
    has_sink = sink_ref is not None
    with_lse = put_lse is not None
    vt = v.T
    nk = k.shape[0]
    row = lax.broadcasted_iota(jnp.int32, (nk, BLK), 0)
    col = lax.broadcasted_iota(jnp.int32, (nk, BLK), 1)
    da = col - row + (nk - BLK)
    near = (da <= BLK) if inclusive else (da < BLK)
    if has_prev:
        exists = jnp.where(row >= BLK, 1, jnp.where(a > 0, 1, 0))
    else:
        exists = 1
    ok = jnp.where(da >= 0, jnp.where(near, exists, 0), 0)
    dm = jnp.where(ok > 0, (da * dil).astype(F32), BIG)
    pairs = group // 2
    logits = []
    for kv in range(n_kv):
        ka, kb = _split_heads_k(k[:, (kv // 2) * BLK:(kv // 2 + 1) * BLK], kv % 2 == 1)
        qst = jnp.concatenate([get_q(kv * pairs + p_) for p_ in range(pairs)], axis=0)
        logits.append((lax.dot_general(ka, qst, NT, preferred_element_type=F32),
                       lax.dot_general(kb, qst, NT, preferred_element_type=F32)))
    accs, inv_all = [], []
    for kv in range(n_kv):
        s_even, s_odd = logits[kv]
        pts, invs = [], []
        for g in range(group):
            p_, odd = divmod(g, 2)
            h = kv * group + g
            s = (s_odd if odd else s_even)[:, p_ * BLK:(p_ + 1) * BLK] - slopes[h] * dm
            m = jnp.max(s, axis=0, keepdims=True)
            if has_sink:
                sink = sink_ref[h] * LOG2E
                m = jnp.maximum(m, sink)
            p = jnp.exp2(s - m)
            den = jnp.sum(p, axis=0, keepdims=True)
            if has_sink:
                den = den + jnp.exp2(sink - m)
            if with_lse:
                put_lse(h, m + jnp.log2(den))
            pts.append(p.astype(BF))
            invs.append(1.0 / den)
        accs.append(jnp.dot(vt[kv * HEAD_DIM:(kv + 1) * HEAD_DIM, :], jnp.concatenate(pts, axis=1),
                            preferred_element_type=F32))
        inv_all.append(invs)
    for kv in range(n_kv):
        acc, invs = accs[kv], inv_all[kv]
        for p_ in range(pairs):
            oe = acc[:, (2 * p_) * BLK:(2 * p_ + 1) * BLK] * invs[2 * p_]
            oo = acc[:, (2 * p_ + 1) * BLK:(2 * p_ + 2) * BLK] * invs[2 * p_ + 1]
            put_o(kv * pairs + p_, jnp.concatenate([oe, oo], axis=0).T)


def _swa_kernel(sink_ref, q_ref, kvp_ref, kvc_ref, o_ref, *, slopes, n_kv, group):
    dkv = n_kv * HEAD_DIM
    k = jnp.concatenate([kvp_ref[:, 0:dkv], kvc_ref[:, 0:dkv]], axis=0)
    v = jnp.concatenate([kvp_ref[:, dkv:2 * dkv], kvc_ref[:, dkv:2 * dkv]], axis=0)

    def put_o(tq, blk):
        o_ref[:, tq * BLK:(tq + 1) * BLK] = blk.astype(o_ref.dtype)

    _band_core(k, v, lambda tq: q_ref[:, tq * BLK:(tq + 1) * BLK], put_o, None, sink_ref,
               pl.program_id(1), slopes=slopes, n_kv=n_kv, group=group, dil=1, inclusive=False,
               has_prev=True)


def _swa_attention(qkv, sinks, *, batch, seq, n_heads, n_kv):
    t = qkv.shape[0]
    nq = seq // BLK
    dq, dkv = n_heads * HEAD_DIM, n_kv * HEAD_DIM
    assert dq % (2 * dkv) == 0 and nq > 1
    kvb = dq // (2 * dkv)
    return pl.pallas_call(
        functools.partial(_swa_kernel, slopes=_alibi_slopes(n_heads), n_kv=n_kv, group=n_heads // n_kv),
        out_shape=jax.ShapeDtypeStruct((t, dq), BF),
        grid=(batch, nq),
        in_specs=[pl.BlockSpec(memory_space=pltpu.SMEM),
                  pl.BlockSpec((BLK, dq), lambda b, i: (b * nq + i, 0)),
                  pl.BlockSpec((BLK, 2 * dkv), lambda b, i: (b * nq + jnp.maximum(i - 1, 0), kvb)),
                  pl.BlockSpec((BLK, 2 * dkv), lambda b, i: (b * nq + i, kvb))],
        out_specs=pl.BlockSpec((BLK, dq), lambda b, i: (b * nq + i, 0)),
        compiler_params=_cparams(("parallel", "parallel")),
        name="swa_attn",
    )(sinks.astype(F32), qkv, qkv, qkv)


def _dil_kernel(*refs, slopes, n_kv, group, dil, has_prev):
    refs = list(refs)
    q_refs = [refs.pop(0), refs.pop(0)]
    kv_refs = [refs.pop(0) for _ in range(2 if has_prev else 1)]
    o_ref, lse_ref, lse_scr = refs
    r = pl.program_id(2)
    rows = pl.ds(r, BLK, stride=dil) if dil > 1 else slice(None)
    ktiles = n_kv * HEAD_DIM // BLK

    def gather(ref, lo, hi):
        return jnp.concatenate([ref[cc, rows, :] for cc in range(lo, hi)], axis=1).astype(BF)

    k = jnp.concatenate([gather(ref, 0, ktiles) for ref in kv_refs], axis=0)
    v = jnp.concatenate([gather(ref, ktiles, 2 * ktiles) for ref in kv_refs], axis=0)
    per_ref = q_refs[0].shape[0]
    lse_scr[...] = jnp.zeros(lse_scr.shape, F32)

    def put_o(tq, blk):
        o_ref[tq, rows, :] = blk

    def put_lse(h, row):
        lse_scr[h:h + 1, :] = row

    _band_core(k, v, lambda tq: q_refs[tq // per_ref][tq % per_ref, rows, :].astype(BF), put_o, put_lse,
               None, pl.program_id(1), slopes=slopes, n_kv=n_kv, group=group, dil=dil, inclusive=True,
               has_prev=has_prev)
    lse_ref[rows, :] = lse_scr[...].T


def _dil_attention(proj_t, g, *, batch, seq, n_heads, n_kv, dil):
    t = proj_t.shape[1]
    cls = seq // dil
    assert seq % dil == 0 and cls % BLK == 0 and n_heads * HEAD_DIM == 8 * BLK and 2 * n_kv * HEAD_DIM == 4 * BLK
    n_a = cls // BLK
    rws = BLK * dil
    has_prev = n_a > 1
    cur = lambda blk: pl.BlockSpec((4, rws, BLK), lambda b, a, r: (blk, b * n_a + a, 0))
    prev = lambda blk: pl.BlockSpec((4, rws, BLK), lambda b, a, r: (blk, b * n_a + jnp.maximum(a - 1, 0), 0))
    specs = [cur(3 * g), cur(3 * g + 1)] + ([prev(3 * g + 2)] if has_prev else []) + [cur(3 * g + 2)]
    return pl.pallas_call(
        functools.partial(_dil_kernel, slopes=_alibi_slopes(n_heads), n_kv=n_kv, group=n_heads // n_kv,
                          dil=dil, has_prev=has_prev),
        out_shape=[jax.ShapeDtypeStruct((8, t, BLK), F32), jax.ShapeDtypeStruct((t, BLK), F32)],
        grid=(batch, n_a, dil),
        in_specs=specs,
        out_specs=[pl.BlockSpec((8, rws, BLK), lambda b, a, r: (0, b * n_a + a, 0)),
                   pl.BlockSpec((rws, BLK), lambda b, a, r: (b * n_a + a, 0))],
        scratch_shapes=[pltpu.VMEM((BLK, BLK), F32)],
        compiler_params=_cparams(("parallel", "parallel", "arbitrary")),
        name=f"dil_attn{g}",
    )(*([proj_t] * len(specs)))


def _dsa_kernel(q_ref, qi_ref, wq_ref, kv_ref, ki_ref, o_ref,
                vt_scr, ka_scr, kb_scr, qst_scr, qist_scr, sc_scr, bound_scr, m_scr, l_scr, acc_scr,
                *, slopes, topk, ch, seq, n_heads, n_idx, hd_dim, idx_dim):
    i = pl.program_id(1)
    t0 = i * BLK
    nch = (t0 + BLK - 1) // ch + 1

    @pl.when(i == 0)
    def _():
        vt_scr[...] = kv_ref[:, hd_dim:2 * hd_dim].T
        ka, kb = _split_heads_k(ki_ref[...], False)
        ka_scr[...] = ka
        kb_scr[...] = kb

    for h in range(n_heads):
        qst_scr[h * BLK:(h + 1) * BLK, :] = q_ref[:, h * hd_dim:(h + 1) * hd_dim]
    for p in range(n_idx // 2):
        qist_scr[p * BLK:(p + 1) * BLK, :] = qi_ref[:, p * BLK:(p + 1) * BLK]

    wscale = float(idx_dim ** -0.5) * float(n_idx ** -0.5)
    wt = wq_ref[...].astype(F32).T * wscale
    row = lax.broadcasted_iota(jnp.int32, (ch, BLK), 0)
    col = lax.broadcasted_iota(jnp.int32, (ch, BLK), 1)
    cr = col - row

    def p1(cc, carry):
        start = pl.multiple_of(cc * ch, ch)
        qi = qist_scr[...]
        rel_e = lax.dot_general(ka_scr[pl.ds(start, ch), :], qi, NT, preferred_element_type=F32)
        rel_o = lax.dot_general(kb_scr[pl.ds(start, ch), :], qi, NT, preferred_element_type=F32)
        score = jnp.zeros((ch, BLK), F32)
        for p in range(n_idx // 2):
            we = wt[idx_dim + 2 * p:idx_dim + 2 * p + 1, :]
            wo = wt[idx_dim + 2 * p + 1:idx_dim + 2 * p + 2, :]
            score = score + we * jnp.maximum(rel_e[:, p * BLK:(p + 1) * BLK], 0.0)
            score = score + wo * jnp.maximum(rel_o[:, p * BLK:(p + 1) * BLK], 0.0)
        sc_scr[pl.ds(start, ch), :] = jnp.where(cr + (t0 - start) >= 0, score, -jnp.inf)
        return carry

    lax.fori_loop(0, nch, p1, 0)

    tq = t0 + lax.broadcasted_iota(jnp.int32, (1, BLK), 1)
    kq = jnp.minimum(tq + 1, topk).astype(F32)

    def key_to_f32(key):
        bits = jnp.where(key < 0, key ^ jnp.int32(0x7FFFFFFF), key)
        return lax.bitcast_convert_type(bits, F32)

    def count(hit_fn):
        lanes = 64

        def body(cc, acc):
            start = pl.multiple_of(cc * ch, ch)
            hit = hit_fn(sc_scr[pl.ds(start, ch), :], start)
            for j in range(ch // lanes):
                acc = acc + hit[j * lanes:(j + 1) * lanes, :]
            return acc

        acc = lax.fori_loop(0, nch, body, jnp.zeros((lanes, BLK), F32))
        return jnp.sum(acc, axis=0, keepdims=True)

    def count_ge(cf):
        return count(lambda blk, start: jnp.where(blk >= cf, 1.0, 0.0))

    ans0 = jnp.where(count_ge(jnp.zeros((1, BLK), F32)) >= kq,
                     jnp.int32(0), jnp.int32(-2147483648))

    def radix(b, ans):
        cand = ans | lax.shift_left(jnp.int32(1), 30 - b)
        return jnp.where(count_ge(key_to_f32(cand)) >= kq, cand, ans)

    thr = key_to_f32(lax.fori_loop(0, 31, radix, ans0))

    bound_scr[...] = jnp.full(bound_scr.shape, seq, jnp.int32)

    @pl.when(jnp.max(count_ge(thr) - kq) > 0.0)
    def _():
        need = kq - count(lambda blk, start: jnp.where(blk > thr, 1.0, 0.0))

        def ties_below(x):
            return count(lambda blk, start: jnp.where(blk == thr, jnp.where(row + start < x, 1.0, 0.0), 0.0))

        nbits = (seq - 1).bit_length()

        def bsearch(b, x):
            cand = x | lax.shift_left(jnp.int32(1), nbits - 1 - b)
            return jnp.where(ties_below(cand) < need, cand, x)

        bound_scr[...] = lax.fori_loop(0, nbits, bsearch, jnp.zeros((1, BLK), jnp.int32)) + 1

    def p3(cc, carry):
        start = pl.multiple_of(cc * ch, ch)
        blk = sc_scr[pl.ds(start, ch), :]
        tie_ok = jnp.where(row + start < bound_scr[...], 1, 0)
        keep = jnp.where(blk > thr, 1, jnp.where(blk == thr, tie_ok, 0))
        sc_scr[pl.ds(start, ch), :] = jnp.where(keep > 0, (cr + (t0 - start)).astype(F32), BIG)
        return carry

    lax.fori_loop(0, nch, p3, 0)

    m_scr[...] = jnp.full(m_scr.shape, NEG, F32)
    l_scr[...] = jnp.zeros(l_scr.shape, F32)
    acc_scr[...] = jnp.zeros(acc_scr.shape, F32)

    def p4(cc, carry):
        start = pl.multiple_of(cc * ch, ch)
        s_all = lax.dot_general(kv_ref[pl.ds(start, ch), 0:hd_dim], qst_scr[...], NT,
                                preferred_element_type=F32)
        dm = sc_scr[pl.ds(start, ch), :]
        pts, alphas = [], []
        for h in range(n_heads):
            hs = slice(h * BLK, (h + 1) * BLK)
            s = s_all[:, hs] - slopes[h] * dm
            m_old = m_scr[:, hs]
            m_new = jnp.maximum(m_old, jnp.max(s, axis=0, keepdims=True))
            alpha = jnp.exp2(m_old - m_new)
            p = jnp.exp2(s - m_new)
            l_scr[:, hs] = alpha * l_scr[:, hs] + jnp.sum(p, axis=0, keepdims=True)
            m_scr[:, hs] = m_new
            pts.append(p.astype(BF))
            alphas.append(alpha)
        pv = jnp.dot(vt_scr[:, pl.ds(start, ch)], jnp.concatenate(pts, axis=1),
                     preferred_element_type=F32)
        acc_scr[...] = jnp.concatenate(alphas, axis=1) * acc_scr[...] + pv
        return carry

    lax.fori_loop(0, nch, p4, 0)
    for h in range(n_heads):
        hs = slice(h * BLK, (h + 1) * BLK)
        o_ref[:, h * hd_dim:(h + 1) * hd_dim] = (acc_scr[:, hs] * (1.0 / l_scr[:, hs])).T.astype(o_ref.dtype)


def _dsa_attention(qq, proj, kv, *, batch, seq, n_heads, hd_dim, n_idx, idx_dim, kidx_block):
    t = qq.shape[0]
    nq = seq // BLK
    dq = n_heads * hd_dim
    di = n_idx * idx_dim
    assert dq % di == 0 and hd_dim == BLK and idx_dim == HEAD_DIM
    topk = min(TOPK_MAX, seq // 4)
    ch = min(512, seq)
    return pl.pallas_call(
        functools.partial(_dsa_kernel, slopes=_alibi_slopes(n_heads), topk=topk, ch=ch, seq=seq,
                          n_heads=n_heads, n_idx=n_idx, hd_dim=hd_dim, idx_dim=idx_dim),
        out_shape=jax.ShapeDtypeStruct((t, dq), BF),
        grid=(batch, nq),
        in_specs=[pl.BlockSpec((BLK, dq), lambda b, i: (b * nq + i, 0)),
                  pl.BlockSpec((BLK, di), lambda b, i: (b * nq + i, dq // di)),
                  pl.BlockSpec((BLK, BLK), lambda b, i: (b * nq + i, kidx_block)),
                  pl.BlockSpec((seq, 2 * hd_dim), lambda b, i: (b, 0)),
                  pl.BlockSpec((seq, BLK), lambda b, i: (b, kidx_block))],
        out_specs=pl.BlockSpec((BLK, dq), lambda b, i: (b * nq + i, 0)),
        scratch_shapes=[pltpu.VMEM((hd_dim, seq), BF),
                        pltpu.VMEM((seq, BLK), BF),
                        pltpu.VMEM((seq, BLK), BF),
                        pltpu.VMEM((n_heads * BLK, hd_dim), BF),
                        pltpu.VMEM((n_idx // 2 * BLK, BLK), BF),
                        pltpu.VMEM((seq, BLK), F32),
                        pltpu.VMEM((1, BLK), jnp.int32),
                        pltpu.VMEM((1, n_heads * BLK), F32),
                        pltpu.VMEM((1, n_heads * BLK), F32),
                        pltpu.VMEM((hd_dim, n_heads * BLK), F32)],
        compiler_params=_cparams(("parallel", "arbitrary")),
        name="dsa_attn",
    )(qq, qq, proj, kv, proj)


def _mixer_swa(xf, pro, gate, nxt, w_in, w_out, j, qn_g, kn_g, sinks, *, batch, seq):
    d = xf.shape[1]
    n_heads = d // HEAD_DIM
    n_kv = n_heads // 8
    dq, dkv = n_heads * HEAD_DIM, n_kv * HEAD_DIM
    gs = np.concatenate([np.full(dq + dkv, HEAD_DIM), np.zeros(dkv)]).astype(np.int64)
    gain = jnp.concatenate([jnp.tile(qn_g * (HEAD_DIM ** -0.5 * LOG2E), n_heads), jnp.tile(kn_g, n_kv),
                            jnp.ones((dkv,), F32)])
    qkv = _matmul(xf, w_in, j, seq=seq, tm=1024, tn=512, out_dtype=BF, prologue=pro,
                  gnorm=(gs, gain), name="swa_in")
    o = _swa_attention(qkv, sinks, batch=batch, seq=seq, n_heads=n_heads, n_kv=n_kv)
    return _out_proj(o, w_out, j, xf, gate, nxt, seq=seq, tm=512, name="swa_out")


def _mixer_dsa(xf, pro, gate, nxt, w_in, qlat_g, kvlat_g, w_uq, w_ukv, w_idx_q, qn_g, kn_g, w_out, j,
               *, batch, seq):
    qrank, kvrank = qlat_g.shape[0], kvlat_g.shape[0]
    hd_dim = kn_g.shape[0]
    n_heads = w_uq.shape[1] // hd_dim
    n_in = w_in.shape[1]
    n_idx = 16
    idx_dim = n_in - qrank - kvrank - n_idx
    assert idx_dim + n_idx <= 128 and (qrank + kvrank) % 128 == 0
    pad = qrank + kvrank + 128 - n_in
    w_in_p = jnp.pad(w_in, ((0, 0), (0, pad))).astype(BF)[None]
    gs = np.concatenate([np.full(qrank, qrank), np.full(kvrank, kvrank), np.zeros(128)]).astype(np.int64)
    gain = jnp.concatenate([qlat_g, kvlat_g, jnp.ones((128,), F32)])
    proj = _matmul(xf, w_in_p, 0, seq=seq, tm=512, tn=qrank + kvrank + 128, out_dtype=BF, prologue=pro,
                   gnorm=(gs, gain), name="dsa_in")
    w_q = jnp.concatenate([w_uq, w_idx_q], axis=1).astype(BF)[None]
    nq_cols, ni_cols = w_uq.shape[1], w_idx_q.shape[1]
    gs_q = np.concatenate([np.full(nq_cols, hd_dim), np.zeros(ni_cols)]).astype(np.int64)
    gain_q = jnp.concatenate([jnp.tile(qn_g * (hd_dim ** -0.5 * LOG2E), n_heads), jnp.ones((ni_cols,), F32)])
    qq = _matmul(proj, w_q, 0, seq=seq, tm=1024, tn=256, out_dtype=BF, lhs_cols=(0, qrank),
                 gnorm=(gs_q, gain_q), name="dsa_q")
    gs_kv = np.concatenate([np.full(hd_dim, hd_dim), np.zeros(hd_dim)]).astype(np.int64)
    gain_kv = jnp.concatenate([kn_g, jnp.ones((hd_dim,), F32)])
    kv = _matmul(proj, w_ukv[None], 0, seq=seq, tm=1024, tn=2 * hd_dim, out_dtype=BF,
                 lhs_cols=(qrank // kvrank, kvrank), gnorm=(gs_kv, gain_kv), name="dsa_kv")
    o = _dsa_attention(qq, proj, kv, batch=batch, seq=seq, n_heads=n_heads, hd_dim=hd_dim,
                       n_idx=n_idx, idx_dim=idx_dim, kidx_block=(qrank + kvrank) // 128)
    return _out_proj(o, w_out, j, xf, gate, nxt, seq=seq, tm=512, name="dsa_out")


def _mixer_dil(xf, pro, gate, nxt, w_in, w_out, j, qn_g, kn_g, *, batch, seq):
    n_groups = len(DIL_PATTERNS)
    n_heads = w_out.shape[1] // HEAD_DIM
    n_kv = n_heads // 4
    dq, dkv = n_heads * HEAD_DIM, n_kv * HEAD_DIM
    gcols = dq + 2 * dkv
    gs = np.tile(np.concatenate([np.full(dq + dkv, HEAD_DIM), np.zeros(dkv)]), n_groups).astype(np.int64)
    gain = jnp.concatenate([
        jnp.concatenate([jnp.tile(qn_g[g] * (HEAD_DIM ** -0.5 * LOG2E), n_heads), jnp.tile(kn_g[g], n_kv),
                         jnp.ones((dkv,), F32)]) for g in range(n_groups)])
    assert gcols == 12 * BLK
    proj_t = _matmul(xf, w_in, j, seq=seq, tm=1024, tn=512, out_dtype=F32, prologue=pro,
                     gnorm=(gs, gain), tile_major=True, name="dil_in")
    outs, lses = [], []
    for g, (win, dil) in enumerate(DIL_PATTERNS):
        assert win == BLK * dil
        o, lse = _dil_attention(proj_t, g, batch=batch, seq=seq, n_heads=n_heads, n_kv=n_kv, dil=dil)
        outs.append(o)
        lses.append(lse)
    return _out_proj(None, w_out, j, xf, gate, nxt, seq=seq, tm=512, mix=(outs, lses), name="dil_out")


def _mlp(xf, h, gate, w1, w2, i, *, seq):
    u = _matmul(h, w1, i, seq=seq, tm=2048, tn=512, out_dtype=BF, relu2=True, name="mlp_up")
    return _matmul_ktiled_resid(u, w2, i, xf, gate, seq=seq, tm=1024, tn=1024, tk=2048,
                                name="mlp_down")


def kernel(x, c, ada_w, ada_b, norm1_g, norm2_g, mlp_w1, mlp_w2, swa_w_in, swa_qn_g, swa_kn_g, swa_sinks, swa_w_out, dsa_w_in, dsa_qlat_g, dsa_kvlat_g, dsa_w_uq, dsa_w_ukv, dsa_w_idx_q, dsa_qn_g, dsa_kn_g, dsa_w_out, dil_w_in, dil_qn_g, dil_kn_g, dil_w_out):
    batch, seq, d = x.shape
    depth = ada_w.shape[0]
    mod = _adaln_mod(c, ada_w, ada_b)
    xf = x.reshape(batch * seq, d)
    mlp_w2_bf = mlp_w2.astype(BF)
    swa_w_in_bf, dil_w_in_bf = swa_w_in.astype(BF), dil_w_in.astype(BF)
    swa_w_out_bf, dsa_w_out_bf, dil_w_out_bf = (w.astype(BF) for w in (swa_w_out, dsa_w_out, dil_w_out))
    for i in range(depth):
        m = mod[i].reshape(batch, 1, 6 * d)
        sh1, sc1, g1, sh2, sc2, g2 = [m[:, :, q * d:(q + 1) * d] for q in range(6)]
        pro, nxt = (norm1_g[i], sh1, sc1), (norm2_g[i], sh2, sc2)
        kind, j = i % 3, i // 3
        if kind == 0:
            xf, h2 = _mixer_swa(xf, pro, g1, nxt, swa_w_in_bf, swa_w_out_bf, j, swa_qn_g[j], swa_kn_g[j],
                                swa_sinks[j], batch=batch, seq=seq)
        elif kind == 1:
            xf, h2 = _mixer_dsa(xf, pro, g1, nxt, dsa_w_in[j], dsa_qlat_g[j], dsa_kvlat_g[j], dsa_w_uq[j],
                                dsa_w_ukv[j], dsa_w_idx_q[j], dsa_qn_g[j], dsa_kn_g[j], dsa_w_out_bf, j,
                                batch=batch, seq=seq)
        else:
            xf, h2 = _mixer_dil(xf, pro, g1, nxt, dil_w_in_bf, dil_w_out_bf, j, dil_qn_g[j], dil_kn_g[j],
                                batch=batch, seq=seq)
        xf = _mlp(xf, h2, g2, mlp_w1, mlp_w2_bf, i, seq=seq)
    return xf.reshape(batch, seq, d)
```

```python
import functools

import jax
import jax.numpy as jnp
import numpy as np
from jax import lax
from jax.experimental import pallas as pl
from jax.experimental.pallas import tpu as pltpu

F32 = jnp.float32
BF = jnp.bfloat16
NT = (((1,), (1,)), ((), ()))

EPS = 1e-6
NEG = -1e30
BIG = 1e30
LOG2E = 1.4426950408889634
BLK = 128
HEAD_DIM = 64
TOPK_MAX = 256
DIL_PATTERNS = ((128, 1), (512, 4), (2048, 16))
VMEM_LIMIT = 56 * 1024 * 1024


def _alibi_slopes(n):
    return [float(2.0 ** (-8.0 * (i + 1) / n)) * LOG2E for i in range(n)]


def _cparams(sem):
    return pltpu.CompilerParams(dimension_semantics=sem, vmem_limit_bytes=VMEM_LIMIT)


def _mod_kernel(c_ref, w_ref, b_ref, o_ref):
    c = c_ref[...]
    cond = c * (1.0 / (1.0 + jnp.exp(-c)))
    o_ref[0] = jnp.dot(cond.astype(BF), w_ref[0].astype(BF),
                       preferred_element_type=F32) + b_ref[0]


def _adaln_mod(c, ada_w, ada_b):
    depth, d, n = ada_w.shape
    b = c.shape[0]
    tn = 1024
    return pl.pallas_call(
        _mod_kernel,
        out_shape=jax.ShapeDtypeStruct((depth, b, n), F32),
        grid=(depth, n // tn),
        in_specs=[pl.BlockSpec((b, d), lambda l, j: (0, 0)),
                  pl.BlockSpec((1, d, tn), lambda l, j: (l, 0, j)),
                  pl.BlockSpec((1, 1, tn), lambda l, j: (l, 0, j))],
        out_specs=pl.BlockSpec((1, b, tn), lambda l, j: (l, 0, j)),
        compiler_params=_cparams(("parallel", "parallel")),
        name="adaln_mod",
    )(c, ada_w, ada_b.reshape(depth, 1, n))


def _mm_kernel(*refs, prologue, epilogue, tm, tile_major):
    refs = list(refs)
    lhs_ref = refs.pop(0)
    if prologue:
        ng_ref, sh_ref, sc_ref = refs.pop(0), refs.pop(0), refs.pop(0)
    w_ref = refs.pop(0)
    if epilogue == "gnorm":
        gm_ref, gain_ref, flag_ref = refs.pop(0), refs.pop(0), refs.pop(0)
    if epilogue == "resid":
        res_ref, gate_ref = refs.pop(0), refs.pop(0)
    o_ref = refs.pop(0)

    if prologue:
        h_scr = refs.pop(0)
        rows = min(tm, 256)

        @pl.when(pl.program_id(1) == 0)
        def _():
            gmul = ng_ref[...] * (1.0 + sc_ref[0])
            shift = sh_ref[0]

            def body(rc, carry):
                r0 = pl.multiple_of(rc * rows, rows)
                xb = lhs_ref[pl.ds(r0, rows), :]
                ms = jnp.mean(xb * xb, axis=-1, keepdims=True)
                h = xb * lax.rsqrt(ms + EPS) * gmul + shift
                h_scr[pl.ds(r0, rows), :] = h.astype(BF)
                return carry

            lax.fori_loop(0, tm // rows, body, 0)

        lhs = h_scr[...]
    else:
        lhs = lhs_ref[...]

    acc = jnp.dot(lhs, w_ref[0].astype(BF), preferred_element_type=F32)
    if epilogue == "gnorm":
        ms = jnp.dot((acc * acc).astype(BF), gm_ref[0], preferred_element_type=F32)
        scale = jnp.where(flag_ref[...] > 0.0, lax.rsqrt(ms + EPS) * gain_ref[...], 1.0)
        out = acc * scale
    elif epilogue == "relu2":
        r = jnp.maximum(acc, 0.0)
        out = r * r
    elif epilogue == "resid":
        out = res_ref[...] + gate_ref[0] * acc
    else:
        out = acc
    if tile_major:
        for cc in range(out.shape[1] // BLK):
            o_ref[cc] = out[:, cc * BLK:(cc + 1) * BLK].astype(o_ref.dtype)
    else:
        o_ref[...] = out.astype(o_ref.dtype)


def _group_mats(gs, tn):
    n = gs.shape[0]
    col = np.arange(n)
    gsafe = np.maximum(gs, 1)
    grp = col // gsafe
    r = np.arange(tn)
    out = np.zeros((n // tn, tn, tn), np.float32)
    for j in range(n // tn):
        cj = col[j * tn:(j + 1) * tn]
        same = (grp[cj][None, :] == ((r[:, None] + j * tn) // gsafe[cj][None, :]))
        val = np.where(gs[cj] > 0, 1.0 / gsafe[cj], 0.0)[None, :]
        out[j] = np.where(same, val, 0.0)
    return jnp.asarray(out, dtype=BF)


def _matmul(lhs, w, wl, *, seq, tm, tn, out_dtype, lhs_cols=None, prologue=None,
            gnorm=None, relu2=False, resid=None, tile_major=False, name="mm"):
    t = lhs.shape[0]
    _, k, n = w.shape
    tm = min(tm, seq)
    cb = 0 if lhs_cols is None else lhs_cols[0]
    nb = seq // tm
    args, specs = [lhs], [pl.BlockSpec((tm, k), lambda i, j: (i, cb))]
    if prologue is not None:
        ng, sh, sc = prologue
        args += [ng.reshape(1, k), sh, sc]
        specs += [pl.BlockSpec((1, k), lambda i, j: (0, 0)),
                  pl.BlockSpec((1, 1, k), lambda i, j: (i // nb, 0, 0)),
                  pl.BlockSpec((1, 1, k), lambda i, j: (i // nb, 0, 0))]
    args.append(w)
    specs.append(pl.BlockSpec((1, k, tn), lambda i, j: (wl, 0, j)))
    epilogue = "none"
    if gnorm is not None:
        epilogue = "gnorm"
        gs, gain = gnorm
        flag = jnp.asarray((gs > 0).astype(np.float32)).reshape(1, n)
        args += [_group_mats(gs, tn), gain.reshape(1, n).astype(F32), flag]
        specs += [pl.BlockSpec((1, tn, tn), lambda i, j: (j, 0, 0)),
                  pl.BlockSpec((1, tn), lambda i, j: (0, j)),
                  pl.BlockSpec((1, tn), lambda i, j: (0, j))]
    if relu2:
        epilogue = "relu2"
    if resid is not None:
        epilogue = "resid"
        res, gate = resid
        args += [res, gate]
        specs += [pl.BlockSpec((tm, tn), lambda i, j: (i, j)),
                  pl.BlockSpec((1, 1, tn), lambda i, j: (i // nb, 0, j))]
    scratch = [pltpu.VMEM((tm, k), BF)] if prologue is not None else []
    if tile_major:
        out_shape = jax.ShapeDtypeStruct((n // BLK, t, BLK), out_dtype)
        out_spec = pl.BlockSpec((tn // BLK, tm, BLK), lambda i, j: (j, i, 0))
    else:
        out_shape = jax.ShapeDtypeStruct((t, n), out_dtype)
        out_spec = pl.BlockSpec((tm, tn), lambda i, j: (i, j))
    return pl.pallas_call(
        functools.partial(_mm_kernel, prologue=prologue is not None, epilogue=epilogue, tm=tm,
                          tile_major=tile_major),
        out_shape=out_shape,
        grid=(t // tm, n // tn),
        in_specs=specs,
        out_specs=out_spec,
        scratch_shapes=scratch,
        compiler_params=_cparams(("parallel", "arbitrary")),
        name=name,
    )(*args)


def _out_kernel(*refs, n_mix):
    refs = list(refs)
    if n_mix:
        o_refs = [refs.pop(0) for _ in range(n_mix)]
        l_refs = [refs.pop(0) for _ in range(n_mix)]
        e_ref = refs.pop(0)
    else:
        lhs_ref = refs.pop(0)
    w_ref, res_ref, gate_ref, ng_ref, sh_ref, sc_ref, out_ref, h_ref = refs
    if n_mix:
        ls = [r[...] for r in l_refs]
        mx = functools.reduce(jnp.maximum, ls)
        es = [jnp.exp2(l - mx) for l in ls]
        inv = 1.0 / functools.reduce(lambda a, b: a + b, es)
        e = e_ref[...]
        lhs = None
        for o_ref, eg in zip(o_refs, es):
            wg = eg * inv
            hi = wg.astype(BF)
            lo = (wg - hi.astype(F32)).astype(BF)
            wfull = (jnp.dot(hi, e, preferred_element_type=F32)
                     + jnp.dot(lo, e, preferred_element_type=F32))
            og = jnp.concatenate([o_ref[cc] for cc in range(o_ref.shape[0])], axis=1)
            term = wfull * og
            lhs = term if lhs is None else lhs + term
        lhs = lhs.astype(BF)
    else:
        lhs = lhs_ref[...]
    acc = jnp.dot(lhs, w_ref[0].astype(BF), preferred_element_type=F32)
    x_new = res_ref[...] + gate_ref[0] * acc
    out_ref[...] = x_new
    ms = jnp.mean(x_new * x_new, axis=-1, keepdims=True)
    h = x_new * lax.rsqrt(ms + EPS) * (ng_ref[...] * (1.0 + sc_ref[0])) + sh_ref[0]
    h_ref[...] = h.astype(h_ref.dtype)


def _out_proj(lhs, w, wl, res, gate, nxt, *, seq, tm, mix=None, name):
    _, k, n = w.shape
    t = res.shape[0]
    tm = min(tm, seq)
    nb = seq // tm
    row = lambda width: pl.BlockSpec((tm, width), lambda i: (i, 0))
    if mix is not None:
        outs, lses = mix
        heads = k // HEAD_DIM
        e = np.zeros((BLK, k), np.float32)
        e[np.arange(k) // HEAD_DIM, np.arange(k)] = 1.0
        assert heads <= BLK
        args = list(outs) + list(lses) + [jnp.asarray(e, dtype=BF)]
        tiles = pl.BlockSpec((k // BLK, tm, BLK), lambda i: (0, i, 0))
        specs = [tiles] * len(outs) + [row(BLK)] * len(lses) + [pl.BlockSpec((BLK, k), lambda i: (0, 0))]
        n_mix = len(outs)
    else:
        args, specs, n_mix = [lhs], [row(k)], 0
    ng, sh, sc = nxt
    per_batch = pl.BlockSpec((1, 1, n), lambda i: (i // nb, 0, 0))
    args += [w, res, gate, ng.reshape(1, n), sh, sc]
    specs += [pl.BlockSpec((1, k, n), lambda i: (wl, 0, 0)),
              row(n),
              per_batch,
              pl.BlockSpec((1, n), lambda i: (0, 0)),
              per_batch,
              per_batch]
    return pl.pallas_call(
        functools.partial(_out_kernel, n_mix=n_mix),
        out_shape=[jax.ShapeDtypeStruct((t, n), F32), jax.ShapeDtypeStruct((t, n), BF)],
        grid=(t // tm,),
        in_specs=specs,
        out_specs=[row(n), row(n)],
        compiler_params=_cparams(("parallel",)),
        name=name,
    )(*args)


def _split_heads_k(tile, odd):
    nk = tile.shape[0]
    z = jnp.zeros((nk, HEAD_DIM), tile.dtype)
    half = tile[:, HEAD_DIM:] if odd else tile[:, :HEAD_DIM]
    return jnp.concatenate([half, z], axis=1), jnp.concatenate([z, half], axis=1)


def _band_core(k, v, get_q, put_o, put_lse, sink_ref, a, *, slopes, n_kv, group, dil, inclusive,
               has_prev):
    has_sink = sink_ref is not None
    with_lse = put_lse is not None
    vt = v.T
    nk = k.shape[0]
    row = lax.broadcasted_iota(jnp.int32, (nk, BLK), 0)
    col = lax.broadcasted_iota(jnp.int32, (nk, BLK), 1)
    da = col - row + (nk - BLK)
    near = (da <= BLK) if inclusive else (da < BLK)
    if has_prev:
        exists = jnp.where(row >= BLK, 1, jnp.where(a > 0, 1, 0))
    else:
        exists = 1
    ok = jnp.where(da >= 0, jnp.where(near, exists, 0), 0)
    dm = jnp.where(ok > 0, (da * dil).astype(F32), BIG)
    pairs = group // 2
    logits = []
    for kv in range(n_kv):
        ka, kb = _split_heads_k(k[:, (kv // 2) * BLK:(kv // 2 + 1) * BLK], kv % 2 == 1)
        qst = jnp.concatenate([get_q(kv * pairs + p_) for p_ in range(pairs)], axis=0)
        logits.append((lax.dot_general(ka, qst, NT, preferred_element_type=F32),
                       lax.dot_general(kb, qst, NT, preferred_element_type=F32)))
    accs, inv_all = [], []
    for kv in range(n_kv):
        s_even, s_odd = logits[kv]
        pts, invs = [], []
        for g in range(group):
            p_, odd = divmod(g, 2)
            h = kv * group + g
            s = (s_odd if odd else s_even)[:, p_ * BLK:(p_ + 1) * BLK] - slopes[h] * dm
            m = jnp.max(s, axis=0, keepdims=True)
            if has_sink:
                sink = sink_ref[h] * LOG2E
                m = jnp.maximum(m, sink)
            p = jnp.exp2(s - m)
            den = jnp.sum(p, axis=0, keepdims=True)
            if has_sink:
                den = den + jnp.exp2(sink - m)
            if with_lse:
                put_lse(h, m + jnp.log2(den))
            pts.append(p.astype(BF))
            invs.append(1.0 / den)
        accs.append(jnp.dot(vt[kv * HEAD_DIM:(kv + 1) * HEAD_DIM, :], jnp.concatenate(pts, axis=1),
                            preferred_element_type=F32))
        inv_all.append(invs)
    for kv in range(n_kv):
        acc, invs = accs[kv], inv_all[kv]
        for p_ in range(pairs):
            oe = acc[:, (2 * p_) * BLK:(2 * p_ + 1) * BLK] * invs[2 * p_]
            oo = acc[:, (2 * p_ + 1) * BLK:(2 * p_ + 2) * BLK] * invs[2 * p_ + 1]
            put_o(kv * pairs + p_, jnp.concatenate([oe, oo], axis=0).T)


def _swa_kernel(sink_ref, q_ref, kvp_ref, kvc_ref, o_ref, *, slopes, n_kv, group):
    dkv = n_kv * HEAD_DIM
    k = jnp.concatenate([kvp_ref[:, 0:dkv], kvc_ref[:, 0:dkv]], axis=0)
    v = jnp.concatenate([kvp_ref[:, dkv:2 * dkv], kvc_ref[:, dkv:2 * dkv]], axis=0)

    def put_o(tq, blk):
        o_ref[:, tq * BLK:(tq + 1) * BLK] = blk.astype(o_ref.dtype)

    _band_core(k, v, lambda tq: q_ref[:, tq * BLK:(tq + 1) * BLK], put_o, None, sink_ref,
               pl.program_id(1), slopes=slopes, n_kv=n_kv, group=group, dil=1, inclusive=False,
               has_prev=True)


def _swa_attention(qkv, sinks, *, batch, seq, n_heads, n_kv):
    t = qkv.shape[0]
    nq = seq // BLK
    dq, dkv = n_heads * HEAD_DIM, n_kv * HEAD_DIM
    assert dq % (2 * dkv) == 0 and nq > 1
    kvb = dq // (2 * dkv)
    return pl.pallas_call(
        functools.partial(_swa_kernel, slopes=_alibi_slopes(n_heads), n_kv=n_kv, group=n_heads // n_kv),
        out_shape=jax.ShapeDtypeStruct((t, dq), BF),
        grid=(batch, nq),
        in_specs=[pl.BlockSpec(memory_space=pltpu.SMEM),
                  pl.BlockSpec((BLK, dq), lambda b, i: (b * nq + i, 0)),
                  pl.BlockSpec((BLK, 2 * dkv), lambda b, i: (b * nq + jnp.maximum(i - 1, 0), kvb)),
                  pl.BlockSpec((BLK, 2 * dkv), lambda b, i: (b * nq + i, kvb))],
        out_specs=pl.BlockSpec((BLK, dq), lambda b, i: (b * nq + i, 0)),
        compiler_params=_cparams(("parallel", "parallel")),
        name="swa_attn",
    )(sinks.astype(F32), qkv, qkv, qkv)


def _dil_kernel(*refs, slopes, n_kv, group, dil, has_prev):
    refs = list(refs)
    q_refs = [refs.pop(0), refs.pop(0)]
    kv_refs = [refs.pop(0) for _ in range(2 if has_prev else 1)]
    o_ref, lse_ref, lse_scr = refs
    r = pl.program_id(2)
    rows = pl.ds(r, BLK, stride=dil) if dil > 1 else slice(None)
    ktiles = n_kv * HEAD_DIM // BLK

    def gather(ref, lo, hi):
        return jnp.concatenate([ref[cc, rows, :] for cc in range(lo, hi)], axis=1).astype(BF)

    k = jnp.concatenate([gather(ref, 0, ktiles) for ref in kv_refs], axis=0)
    v = jnp.concatenate([gather(ref, ktiles, 2 * ktiles) for ref in kv_refs], axis=0)
    per_ref = q_refs[0].shape[0]
    lse_scr[...] = jnp.zeros(lse_scr.shape, F32)

    def put_o(tq, blk):
        o_ref[tq, rows, :] = blk

    def put_lse(h, row):
        lse_scr[h:h + 1, :] = row

    _band_core(k, v, lambda tq: q_refs[tq // per_ref][tq % per_ref, rows, :].astype(BF), put_o, put_lse,
               None, pl.program_id(1), slopes=slopes, n_kv=n_kv, group=group, dil=dil, inclusive=True,
               has_prev=has_prev)
    lse_ref[rows, :] = lse_scr[...].T


def _dil_attention(proj_t, g, *, batch, seq, n_heads, n_kv, dil):
    t = proj_t.shape[1]
    cls = seq // dil
    assert seq % dil == 0 and cls % BLK == 0 and n_heads * HEAD_DIM == 8 * BLK and 2 * n_kv * HEAD_DIM == 4 * BLK
    n_a = cls // BLK
    rws = BLK * dil
    has_prev = n_a > 1
    cur = lambda blk: pl.BlockSpec((4, rws, BLK), lambda b, a, r: (blk, b * n_a + a, 0))
    prev = lambda blk: pl.BlockSpec((4, rws, BLK), lambda b, a, r: (blk, b * n_a + jnp.maximum(a - 1, 0), 0))
    specs = [cur(3 * g), cur(3 * g + 1)] + ([prev(3 * g + 2)] if has_prev else []) + [cur(3 * g + 2)]
    return pl.pallas_call(
        functools.partial(_dil_kernel, slopes=_alibi_slopes(n_heads), n_kv=n_kv, group=n_heads // n_kv,
                          dil=dil, has_prev=has_prev),
        out_shape=[jax.ShapeDtypeStruct((8, t, BLK), F32), jax.ShapeDtypeStruct((t, BLK), F32)],
        grid=(batch, n_a, dil),
        in_specs=specs,
        out_specs=[pl.BlockSpec((8, rws, BLK), lambda b, a, r: (0, b * n_a + a, 0)),
                   pl.BlockSpec((rws, BLK), lambda b, a, r: (b * n_a + a, 0))],
        scratch_shapes=[pltpu.VMEM((BLK, BLK), F32)],
        compiler_params=_cparams(("parallel", "parallel", "arbitrary")),
        name=f"dil_attn{g}",
    )(*([proj_t] * len(specs)))


def _dsa_kernel(q_ref, qi_ref, wq_ref, kv_ref, ki_ref, o_ref,
                vt_scr, ka_scr, kb_scr, qst_scr, qist_scr, sc_scr, bound_scr, m_scr, l_scr, acc_scr,
                *, slopes, topk, ch, seq, n_heads, n_idx, hd_dim, idx_dim):
    i = pl.program_id(1)
    t0 = i * BLK
    nch = (t0 + BLK - 1) // ch + 1

    @pl.when(i == 0)
    def _():
        vt_scr[...] = kv_ref[:, hd_dim:2 * hd_dim].T
        ka, kb = _split_heads_k(ki_ref[...], False)
        ka_scr[...] = ka
        kb_scr[...] = kb

    for h in range(n_heads):
        qst_scr[h * BLK:(h + 1) * BLK, :] = q_ref[:, h * hd_dim:(h + 1) * hd_dim]
    for p in range(n_idx // 2):
        qist_scr[p * BLK:(p + 1) * BLK, :] = qi_ref[:, p * BLK:(p + 1) * BLK]

    wscale = float(idx_dim ** -0.5) * float(n_idx ** -0.5)
    wt = wq_ref[...].astype(F32).T * wscale
    row = lax.broadcasted_iota(jnp.int32, (ch, BLK), 0)
    col = lax.broadcasted_iota(jnp.int32, (ch, BLK), 1)
    cr = col - row

    def p1(cc, carry):
        start = pl.multiple_of(cc * ch, ch)
        qi = qist_scr[...]
        rel_e = lax.dot_general(ka_scr[pl.ds(start, ch), :], qi, NT, preferred_element_type=F32)
        rel_o = lax.dot_general(kb_scr[pl.ds(start, ch), :], qi, NT, preferred_element_type=F32)
        score = jnp.zeros((ch, BLK), F32)
        for p in range(n_idx // 2):
            we = wt[idx_dim + 2 * p:idx_dim + 2 * p + 1, :]
            wo = wt[idx_dim + 2 * p + 1:idx_dim + 2 * p + 2, :]
            score = score + we * jnp.maximum(rel_e[:, p * BLK:(p + 1) * BLK], 0.0)
            score = score + wo * jnp.maximum(rel_o[:, p * BLK:(p + 1) * BLK], 0.0)
        sc_scr[pl.ds(start, ch), :] = jnp.where(cr + (t0 - start) >= 0, score, -jnp.inf)
        return carry

    lax.fori_loop(0, nch, p1, 0)

    tq = t0 + lax.broadcasted_iota(jnp.int32, (1, BLK), 1)
    kq = jnp.minimum(tq + 1, topk).astype(F32)

    def key_to_f32(key):
        bits = jnp.where(key < 0, key ^ jnp.int32(0x7FFFFFFF), key)
        return lax.bitcast_convert_type(bits, F32)

    def count(hit_fn):
        lanes = 64

        def body(cc, acc):
            start = pl.multiple_of(cc * ch, ch)
            hit = hit_fn(sc_scr[pl.ds(start, ch), :], start)
            for j in range(ch // lanes):
                acc = acc + hit[j * lanes:(j + 1) * lanes, :]
            return acc

        acc = lax.fori_loop(0, nch, body, jnp.zeros((lanes, BLK), F32))
        return jnp.sum(acc, axis=0, keepdims=True)

    def count_ge(cf):
        return count(lambda blk, start: jnp.where(blk >= cf, 1.0, 0.0))

    ans0 = jnp.where(count_ge(jnp.zeros((1, BLK), F32)) >= kq,
                     jnp.int32(0), jnp.int32(-2147483648))

    def radix(b, ans):
        cand = ans | lax.shift_left(jnp.int32(1), 30 - b)
        return jnp.where(count_ge(key_to_f32(cand)) >= kq, cand, ans)

    thr = key_to_f32(lax.fori_loop(0, 31, radix, ans0))

    bound_scr[...] = jnp.full(bound_scr.shape, seq, jnp.int32)

    @pl.when(jnp.max(count_ge(thr) - kq) > 0.0)
    def _():
        need = kq - count(lambda blk, start: jnp.where(blk > thr, 1.0, 0.0))

        def ties_below(x):
            return count(lambda blk, start: jnp.where(blk == thr, jnp.where(row + start < x, 1.0, 0.0), 0.0))

        nbits = (seq - 1).bit_length()

        def bsearch(b, x):
            cand = x | lax.shift_left(jnp.int32(1), nbits - 1 - b)
            return jnp.where(ties_below(cand) < need, cand, x)

        bound_scr[...] = lax.fori_loop(0, nbits, bsearch, jnp.zeros((1, BLK), jnp.int32)) + 1

    def p3(cc, carry):
        start = pl.multiple_of(cc * ch, ch)
        blk = sc_scr[pl.ds(start, ch), :]
        tie_ok = jnp.where(row + start < bound_scr[...], 1, 0)
        keep = jnp.where(blk > thr, 1, jnp.where(blk == thr, tie_ok, 0))
        sc_scr[pl.ds(start, ch), :] = jnp.where(keep > 0, (cr + (t0 - start)).astype(F32), BIG)
        return carry

    lax.fori_loop(0, nch, p3, 0)

    m_scr[...] = jnp.full(m_scr.shape, NEG, F32)
    l_scr[...] = jnp.zeros(l_scr.shape, F32)
    acc_scr[...] = jnp.zeros(acc_scr.shape, F32)

    def p4(cc, carry):
        start = pl.multiple_of(cc * ch, ch)
        s_all = lax.dot_general(kv_ref[pl.ds(start, ch), 0:hd_dim], qst_scr[...], NT,
                                preferred_element_type=F32)
        dm = sc_scr[pl.ds(start, ch), :]
        pts, alphas = [], []
        for h in range(n_heads):
            hs = slice(h * BLK, (h + 1) * BLK)
            s = s_all[:, hs] - slopes[h] * dm
            m_old = m_scr[:, hs]
            m_new = jnp.maximum(m_old, jnp.max(s, axis=0, keepdims=True))
            alpha = jnp.exp2(m_old - m_new)
            p = jnp.exp2(s - m_new)
            l_scr[:, hs] = alpha * l_scr[:, hs] + jnp.sum(p, axis=0, keepdims=True)
            m_scr[:, hs] = m_new
            pts.append(p.astype(BF))
            alphas.append(alpha)
        pv = jnp.dot(vt_scr[:, pl.ds(start, ch)], jnp.concatenate(pts, axis=1),
                     preferred_element_type=F32)
        acc_scr[...] = jnp.concatenate(alphas, axis=1) * acc_scr[...] + pv
        return carry

    lax.fori_loop(0, nch, p4, 0)
    for h in range(n_heads):
        hs = slice(h * BLK, (h + 1) * BLK)
        o_ref[:, h * hd_dim:(h + 1) * hd_dim] = (acc_scr[:, hs] * (1.0 / l_scr[:, hs])).T.astype(o_ref.dtype)


def _dsa_attention(qq, proj, kv, *, batch, seq, n_heads, hd_dim, n_idx, idx_dim, kidx_block):
    t = qq.shape[0]
    nq = seq // BLK
    dq = n_heads * hd_dim
    di = n_idx * idx_dim
    assert dq % di == 0 and hd_dim == BLK and idx_dim == HEAD_DIM
    topk = min(TOPK_MAX, seq // 4)
    ch = min(512, seq)
    return pl.pallas_call(
        functools.partial(_dsa_kernel, slopes=_alibi_slopes(n_heads), topk=topk, ch=ch, seq=seq,
                          n_heads=n_heads, n_idx=n_idx, hd_dim=hd_dim, idx_dim=idx_dim),
        out_shape=jax.ShapeDtypeStruct((t, dq), BF),
        grid=(batch, nq),
        in_specs=[pl.BlockSpec((BLK, dq), lambda b, i: (b * nq + i, 0)),
                  pl.BlockSpec((BLK, di), lambda b, i: (b * nq + i, dq // di)),
                  pl.BlockSpec((BLK, BLK), lambda b, i: (b * nq + i, kidx_block)),
                  pl.BlockSpec((seq, 2 * hd_dim), lambda b, i: (b, 0)),
                  pl.BlockSpec((seq, BLK), lambda b, i: (b, kidx_block))],
        out_specs=pl.BlockSpec((BLK, dq), lambda b, i: (b * nq + i, 0)),
        scratch_shapes=[pltpu.VMEM((hd_dim, seq), BF),
                        pltpu.VMEM((seq, BLK), BF),
                        pltpu.VMEM((seq, BLK), BF),
                        pltpu.VMEM((n_heads * BLK, hd_dim), BF),
                        pltpu.VMEM((n_idx // 2 * BLK, BLK), BF),
                        pltpu.VMEM((seq, BLK), F32),
                        pltpu.VMEM((1, BLK), jnp.int32),
                        pltpu.VMEM((1, n_heads * BLK), F32),
                        pltpu.VMEM((1, n_heads * BLK), F32),
                        pltpu.VMEM((hd_dim, n_heads * BLK), F32)],
        compiler_params=_cparams(("parallel", "arbitrary")),
        name="dsa_attn",
    )(qq, qq, proj, kv, proj)


def _mixer_swa(xf, pro, gate, nxt, w_in, w_out, j, qn_g, kn_g, sinks, *, batch, seq):
    d = xf.shape[1]
    n_heads = d // HEAD_DIM
    n_kv = n_heads // 8
    dq, dkv = n_heads * HEAD_DIM, n_kv * HEAD_DIM
    gs = np.concatenate([np.full(dq + dkv, HEAD_DIM), np.zeros(dkv)]).astype(np.int64)
    gain = jnp.concatenate([jnp.tile(qn_g * (HEAD_DIM ** -0.5 * LOG2E), n_heads), jnp.tile(kn_g, n_kv),
                            jnp.ones((dkv,), F32)])
    qkv = _matmul(xf, w_in, j, seq=seq, tm=1024, tn=512, out_dtype=BF, prologue=pro,
                  gnorm=(gs, gain), name="swa_in")
    o = _swa_attention(qkv, sinks, batch=batch, seq=seq, n_heads=n_heads, n_kv=n_kv)
    return _out_proj(o, w_out, j, xf, gate, nxt, seq=seq, tm=512, name="swa_out")


def _mixer_dsa(xf, pro, gate, nxt, w_in, qlat_g, kvlat_g, w_uq, w_ukv, w_idx_q, qn_g, kn_g, w_out, j,
               *, batch, seq):
    qrank, kvrank = qlat_g.shape[0], kvlat_g.shape[0]
    hd_dim = kn_g.shape[0]
    n_heads = w_uq.shape[1] // hd_dim
    n_in = w_in.shape[1]
    n_idx = 16
    idx_dim = n_in - qrank - kvrank - n_idx
    assert idx_dim + n_idx <= 128 and (qrank + kvrank) % 128 == 0
    pad = qrank + kvrank + 128 - n_in
    w_in_p = jnp.pad(w_in, ((0, 0), (0, pad))).astype(BF)[None]
    gs = np.concatenate([np.full(qrank, qrank), np.full(kvrank, kvrank), np.zeros(128)]).astype(np.int64)
    gain = jnp.concatenate([qlat_g, kvlat_g, jnp.ones((128,), F32)])
    proj = _matmul(xf, w_in_p, 0, seq=seq, tm=512, tn=qrank + kvrank + 128, out_dtype=BF, prologue=pro,
                   gnorm=(gs, gain), name="dsa_in")
    w_q = jnp.concatenate([w_uq, w_idx_q], axis=1).astype(BF)[None]
    nq_cols, ni_cols = w_uq.shape[1], w_idx_q.shape[1]
    gs_q = np.concatenate([np.full(nq_cols, hd_dim), np.zeros(ni_cols)]).astype(np.int64)
    gain_q = jnp.concatenate([jnp.tile(qn_g * (hd_dim ** -0.5 * LOG2E), n_heads), jnp.ones((ni_cols,), F32)])
    qq = _matmul(proj, w_q, 0, seq=seq, tm=2048, tn=256, out_dtype=BF, lhs_cols=(0, qrank),
                 gnorm=(gs_q, gain_q), name="dsa_q")
    gs_kv = np.concatenate([np.full(hd_dim, hd_dim), np.zeros(hd_dim)]).astype(np.int64)
    gain_kv = jnp.concatenate([kn_g, jnp.ones((hd_dim,), F32)])
    kv = _matmul(proj, w_ukv[None], 0, seq=seq, tm=1024, tn=2 * hd_dim, out_dtype=BF,
                 lhs_cols=(qrank // kvrank, kvrank), gnorm=(gs_kv, gain_kv), name="dsa_kv")
    o = _dsa_attention(qq, proj, kv, batch=batch, seq=seq, n_heads=n_heads, hd_dim=hd_dim,
                       n_idx=n_idx, idx_dim=idx_dim, kidx_block=(qrank + kvrank) // 128)
    return _out_proj(o, w_out, j, xf, gate, nxt, seq=seq, tm=512, name="dsa_out")


def _mixer_dil(xf, pro, gate, nxt, w_in, w_out, j, qn_g, kn_g, *, batch, seq):
    n_groups = len(DIL_PATTERNS)
    n_heads = w_out.shape[1] // HEAD_DIM
    n_kv = n_heads // 4
    dq, dkv = n_heads * HEAD_DIM, n_kv * HEAD_DIM
    gcols = dq + 2 * dkv
    gs = np.tile(np.concatenate([np.full(dq + dkv, HEAD_DIM), np.zeros(dkv)]), n_groups).astype(np.int64)
    gain = jnp.concatenate([
        jnp.concatenate([jnp.tile(qn_g[g] * (HEAD_DIM ** -0.5 * LOG2E), n_heads), jnp.tile(kn_g[g], n_kv),
                         jnp.ones((dkv,), F32)]) for g in range(n_groups)])
    assert gcols == 12 * BLK
    proj_t = _matmul(xf, w_in, j, seq=seq, tm=1024, tn=512, out_dtype=F32, prologue=pro,
                     gnorm=(gs, gain), tile_major=True, name="dil_in")
    outs, lses = [], []
    for g, (win, dil) in enumerate(DIL_PATTERNS):
        assert win == BLK * dil
        o, lse = _dil_attention(proj_t, g, batch=batch, seq=seq, n_heads=n_heads, n_kv=n_kv, dil=dil)
        outs.append(o)
        lses.append(lse)
    return _out_proj(None, w_out, j, xf, gate, nxt, seq=seq, tm=512, mix=(outs, lses), name="dil_out")


def _mlp(xf, h, gate, w1, w2, i, *, seq):
    u = _matmul(h, w1, i, seq=seq, tm=2048, tn=512, out_dtype=BF, relu2=True, name="mlp_up")
    return _matmul(u, w2, i, seq=seq, tm=512, tn=512, out_dtype=F32, resid=(xf, gate), name="mlp_down")


def kernel(x, c, ada_w, ada_b, norm1_g, norm2_g, mlp_w1, mlp_w2, swa_w_in, swa_qn_g, swa_kn_g, swa_sinks, swa_w_out, dsa_w_in, dsa_qlat_g, dsa_kvlat_g, dsa_w_uq, dsa_w_ukv, dsa_w_idx_q, dsa_qn_g, dsa_kn_g, dsa_w_out, dil_w_in, dil_qn_g, dil_kn_g, dil_w_out):
    batch, seq, d = x.shape
    depth = ada_w.shape[0]
    mod = _adaln_mod(c, ada_w, ada_b)
    xf = x.reshape(batch * seq, d)
    mlp_w2_bf = mlp_w2.astype(BF)
    swa_w_in_bf, dil_w_in_bf = swa_w_in.astype(BF), dil_w_in.astype(BF)
    swa_w_out_bf, dsa_w_out_bf, dil_w_out_bf = (w.astype(BF) for w in (swa_w_out, dsa_w_out, dil_w_out))
    for i in range(depth):
        m = mod[i].reshape(batch, 1, 6 * d)
        sh1, sc1, g1, sh2, sc2, g2 = [m[:, :, q * d:(q + 1) * d] for q in range(6)]
        pro, nxt = (norm1_g[i], sh1, sc1), (norm2_g[i], sh2, sc2)
        kind, j = i % 3, i // 3
        if kind == 0:
            xf, h2 = _mixer_swa(xf, pro, g1, nxt, swa_w_in_bf, swa_w_out_bf, j, swa_qn_g[j], swa_kn_g[j],
                                swa_sinks[j], batch=batch, seq=seq)
        elif kind == 1:
            xf, h2 = _mixer_dsa(xf, pro, g1, nxt, dsa_w_in[j], dsa_qlat_g[j], dsa_kvlat_g[j], dsa_w_uq[j],
                                dsa_w_ukv[j], dsa_w_idx_q[j], dsa_qn_g[j], dsa_kn_g[j], dsa_w_out_bf, j,
                                batch=batch, seq=seq)
        else:
            xf, h2 = _mixer_dil(xf, pro, g1, nxt, dil_w_in_bf, dil_w_out_bf, j, dil_qn_g[j], dil_kn_g[j],
                                batch=batch, seq=seq)
        xf = _mlp(xf, h2, g2, mlp_w1, mlp_w2_bf, i, seq=seq)
    return xf.reshape(batch, seq, d)
```

```python
import functools

import jax
import jax.numpy as jnp
import numpy as np
from jax import lax
from jax.experimental import pallas as pl
from jax.experimental.pallas import tpu as pltpu

F32 = jnp.float32
BF = jnp.bfloat16
NT = (((1,), (1,)), ((), ()))

EPS = 1e-6
NEG = -1e30
BIG = 1e30
LOG2E = 1.4426950408889634
BLK = 128
HEAD_DIM = 64
TOPK_MAX = 256
DIL_PATTERNS = ((128, 1), (512, 4), (2048, 16))
VMEM_LIMIT = 56 * 1024 * 1024


def _alibi_slopes(n):
    return [float(2.0 ** (-8.0 * (i + 1) / n)) * LOG2E for i in range(n)]


def _cparams(sem):
    return pltpu.CompilerParams(dimension_semantics=sem, vmem_limit_bytes=VMEM_LIMIT)


def _mod_kernel(c_ref, w_ref, b_ref, o_ref):
    c = c_ref[...]
    cond = c * (1.0 / (1.0 + jnp.exp(-c)))
    o_ref[0] = jnp.dot(cond.astype(BF), w_ref[0].astype(BF),
                       preferred_element_type=F32) + b_ref[0]


def _adaln_mod(c, ada_w, ada_b):
    depth, d, n = ada_w.shape
    b = c.shape[0]
    tn = 1024
    return pl.pallas_call(
        _mod_kernel,
        out_shape=jax.ShapeDtypeStruct((depth, b, n), F32),
        grid=(depth, n // tn),
        in_specs=[pl.BlockSpec((b, d), lambda l, j: (0, 0)),
                  pl.BlockSpec((1, d, tn), lambda l, j: (l, 0, j)),
                  pl.BlockSpec((1, 1, tn), lambda l, j: (l, 0, j))],
        out_specs=pl.BlockSpec((1, b, tn), lambda l, j: (l, 0, j)),
        compiler_params=_cparams(("parallel", "parallel")),
        name="adaln_mod",
    )(c, ada_w, ada_b.reshape(depth, 1, n))


def _mm_kernel(*refs, prologue, epilogue, tm, tile_major):
    refs = list(refs)
    lhs_ref = refs.pop(0)
    if prologue:
        ng_ref, sh_ref, sc_ref = refs.pop(0), refs.pop(0), refs.pop(0)
    w_ref = refs.pop(0)
    if epilogue == "gnorm":
        gm_ref, gain_ref, flag_ref = refs.pop(0), refs.pop(0), refs.pop(0)
    o_ref = refs.pop(0)

    if prologue:
        h_scr = refs.pop(0)
        rows = min(tm, 256)

        @pl.when(pl.program_id(1) == 0)
        def _():
            gmul = ng_ref[...] * (1.0 + sc_ref[0])
            shift = sh_ref[0]

            def body(rc, carry):
                r0 = pl.multiple_of(rc * rows, rows)
                xb = lhs_ref[pl.ds(r0, rows), :]
                ms = jnp.mean(xb * xb, axis=-1, keepdims=True)
                h = xb * lax.rsqrt(ms + EPS) * gmul + shift
                h_scr[pl.ds(r0, rows), :] = h.astype(BF)
                return carry

            lax.fori_loop(0, tm // rows, body, 0)

        lhs = h_scr[...]
    else:
        lhs = lhs_ref[...]

    acc = jnp.dot(lhs, w_ref[0].astype(BF), preferred_element_type=F32)
    if epilogue == "gnorm":
        ms = jnp.dot((acc * acc).astype(BF), gm_ref[0], preferred_element_type=F32)
        scale = jnp.where(flag_ref[...] > 0.0, lax.rsqrt(ms + EPS) * gain_ref[...], 1.0)
        out = acc * scale
    elif epilogue == "relu2":
        r = jnp.maximum(acc, 0.0)
        out = r * r
    else:
        out = acc
    if tile_major:
        for cc in range(out.shape[1] // BLK):
            o_ref[cc] = out[:, cc * BLK:(cc + 1) * BLK].astype(o_ref.dtype)
    else:
        o_ref[...] = out.astype(o_ref.dtype)


def _group_mats(gs, tn):
    n = gs.shape[0]
    col = np.arange(n)
    gsafe = np.maximum(gs, 1)
    grp = col // gsafe
    r = np.arange(tn)
    out = np.zeros((n // tn, tn, tn), np.float32)
    for j in range(n // tn):
        cj = col[j * tn:(j + 1) * tn]
        same = (grp[cj][None, :] == ((r[:, None] + j * tn) // gsafe[cj][None, :]))
        val = np.where(gs[cj] > 0, 1.0 / gsafe[cj], 0.0)[None, :]
        out[j] = np.where(same, val, 0.0)
    return jnp.asarray(out, dtype=BF)


def _matmul(lhs, w, wl, *, seq, tm, tn, out_dtype, lhs_cols=None, prologue=None,
            gnorm=None, relu2=False, tile_major=False, name="mm"):
    t = lhs.shape[0]
    _, k, n = w.shape
    tm = min(tm, seq)
    cb = 0 if lhs_cols is None else lhs_cols[0]
    nb = seq // tm
    args, specs = [lhs], [pl.BlockSpec((tm, k), lambda i, j: (i, cb))]
    if prologue is not None:
        ng, sh, sc = prologue
        args += [ng.reshape(1, k), sh, sc]
        specs += [pl.BlockSpec((1, k), lambda i, j: (0, 0)),
                  pl.BlockSpec((1, 1, k), lambda i, j: (i // nb, 0, 0)),
                  pl.BlockSpec((1, 1, k), lambda i, j: (i // nb, 0, 0))]
    args.append(w)
    specs.append(pl.BlockSpec((1, k, tn), lambda i, j: (wl, 0, j)))
    epilogue = "none"
    if gnorm is not None:
        epilogue = "gnorm"
        gs, gain = gnorm
        flag = jnp.asarray((gs > 0).astype(np.float32)).reshape(1, n)
        args += [_group_mats(gs, tn), gain.reshape(1, n).astype(F32), flag]
        specs += [pl.BlockSpec((1, tn, tn), lambda i, j: (j, 0, 0)),
                  pl.BlockSpec((1, tn), lambda i, j: (0, j)),
                  pl.BlockSpec((1, tn), lambda i, j: (0, j))]
    if relu2:
        epilogue = "relu2"
    scratch = [pltpu.VMEM((tm, k), BF)] if prologue is not None else []
    if tile_major:
        out_shape = jax.ShapeDtypeStruct((n // BLK, t, BLK), out_dtype)
        out_spec = pl.BlockSpec((tn // BLK, tm, BLK), lambda i, j: (j, i, 0))
    else:
        out_shape = jax.ShapeDtypeStruct((t, n), out_dtype)
        out_spec = pl.BlockSpec((tm, tn), lambda i, j: (i, j))
    return pl.pallas_call(
        functools.partial(_mm_kernel, prologue=prologue is not None, epilogue=epilogue, tm=tm,
                          tile_major=tile_major),
        out_shape=out_shape,
        grid=(t // tm, n // tn),
        in_specs=specs,
        out_specs=out_spec,
        scratch_shapes=scratch,
        compiler_params=_cparams(("parallel", "arbitrary")),
        name=name,
    )(*args)


def _out_kernel(*refs, n_mix):
    refs = list(refs)
    if n_mix:
        o_refs = [refs.pop(0) for _ in range(n_mix)]
        l_refs = [refs.pop(0) for _ in range(n_mix)]
        e_ref = refs.pop(0)
    else:
        lhs_ref = refs.pop(0)
    w_ref, res_ref, gate_ref, ng_ref, sh_ref, sc_ref, out_ref, h_ref = refs
    if n_mix:
        ls = [r[...] for r in l_refs]
        mx = functools.reduce(jnp.maximum, ls)
        es = [jnp.exp2(l - mx) for l in ls]
        inv = 1.0 / functools.reduce(lambda a, b: a + b, es)
        e = e_ref[...]
        lhs = None
        for o_ref, eg in zip(o_refs, es):
            wg = eg * inv
            hi = wg.astype(BF)
            lo = (wg - hi.astype(F32)).astype(BF)
            wfull = (jnp.dot(hi, e, preferred_element_type=F32)
                     + jnp.dot(lo, e, preferred_element_type=F32))
            og = jnp.concatenate([o_ref[cc] for cc in range(o_ref.shape[0])], axis=1)
            term = wfull * og
            lhs = term if lhs is None else lhs + term
        lhs = lhs.astype(BF)
    else:
        lhs = lhs_ref[...]
    acc = jnp.dot(lhs, w_ref[0].astype(BF), preferred_element_type=F32)
    x_new = res_ref[...] + gate_ref[0] * acc
    out_ref[...] = x_new
    ms = jnp.mean(x_new * x_new, axis=-1, keepdims=True)
    h = x_new * lax.rsqrt(ms + EPS) * (ng_ref[...] * (1.0 + sc_ref[0])) + sh_ref[0]
    h_ref[...] = h.astype(h_ref.dtype)


def _out_proj(lhs, w, wl, res, gate, nxt, *, seq, tm, mix=None, name):
    _, k, n = w.shape
    t = res.shape[0]
    tm = min(tm, seq)
    nb = seq // tm
    row = lambda width: pl.BlockSpec((tm, width), lambda i: (i, 0))
    if mix is not None:
        outs, lses = mix
        heads = k // HEAD_DIM
        e = np.zeros((BLK, k), np.float32)
        e[np.arange(k) // HEAD_DIM, np.arange(k)] = 1.0
        assert heads <= BLK
        args = list(outs) + list(lses) + [jnp.asarray(e, dtype=BF)]
        tiles = pl.BlockSpec((k // BLK, tm, BLK), lambda i: (0, i, 0))
        specs = [tiles] * len(outs) + [row(BLK)] * len(lses) + [pl.BlockSpec((BLK, k), lambda i: (0, 0))]
        n_mix = len(outs)
    else:
        args, specs, n_mix = [lhs], [row(k)], 0
    ng, sh, sc = nxt
    per_batch = pl.BlockSpec((1, 1, n), lambda i: (i // nb, 0, 0))
    args += [w, res, gate, ng.reshape(1, n), sh, sc]
    specs += [pl.BlockSpec((1, k, n), lambda i: (wl, 0, 0)),
              row(n),
              per_batch,
              pl.BlockSpec((1, n), lambda i: (0, 0)),
              per_batch,
              per_batch]
    return pl.pallas_call(
        functools.partial(_out_kernel, n_mix=n_mix),
        out_shape=[jax.ShapeDtypeStruct((t, n), F32), jax.ShapeDtypeStruct((t, n), BF)],
        grid=(t // tm,),
        in_specs=specs,
        out_specs=[row(n), row(n)],
        compiler_params=_cparams(("parallel",)),
        name=name,
    )(*args)


def _mmk_kernel(a_ref, w_ref, res_ref, gate_ref, o_ref, acc_ref):
    kk = pl.program_id(2)

    @pl.when(kk == 0)
    def _():
        acc_ref[...] = jnp.zeros_like(acc_ref)

    acc_ref[...] += jnp.dot(a_ref[...], w_ref[0], preferred_element_type=F32)

    @pl.when(kk == pl.num_programs(2) - 1)
    def _():
        o_ref[...] = res_ref[...] + gate_ref[0] * acc_ref[...]


def _matmul_ktiled_resid(a, w, wl, res, gate, *, seq, tm, tn, tk, name):
    t, k = a.shape
    n = w.shape[2]
    tm = min(tm, seq)
    nb = seq // tm
    return pl.pallas_call(
        _mmk_kernel,
        out_shape=jax.ShapeDtypeStruct((t, n), F32),
        grid=(t // tm, n // tn, k // tk),
        in_specs=[pl.BlockSpec((tm, tk), lambda i, j, q: (i, q)),
                  pl.BlockSpec((1, tk, tn), lambda i, j, q: (wl, q, j)),
                  pl.BlockSpec((tm, tn), lambda i, j, q: (i, j)),
                  pl.BlockSpec((1, 1, tn), lambda i, j, q: (i // nb, 0, j))],
        out_specs=pl.BlockSpec((tm, tn), lambda i, j, q: (i, j)),
        scratch_shapes=[pltpu.VMEM((tm, tn), F32)],
        compiler_params=_cparams(("parallel", "parallel", "arbitrary")),
        name=name,
    )(a, w, res, gate)


def _split_heads_k(tile, odd):
    nk = tile.shape[0]
    z = jnp.zeros((nk, HEAD_DIM), tile.dtype)
    half = tile[:, HEAD_DIM:] if odd else tile[:, :HEAD_DIM]
    return jnp.concatenate([half, z], axis=1), jnp.concatenate([z, half], axis=1)


def _band_core(k, v, get_q, put_o, put_lse, sink_ref, a, *, slopes, n_kv, group, dil, inclusive,
               has_prev):
    has_sink = sink_ref is not None
    with_lse = put_lse is not None
    vt = v.T
    nk = k.shape[0]
    row = lax.broadcasted_iota(jnp.int32, (nk, BLK), 0)
    col = lax.broadcasted_iota(jnp.int32, (nk, BLK), 1)
    da = col - row + (nk - BLK)
    near = (da <= BLK) if inclusive else (da < BLK)
    if has_prev:
        exists = jnp.where(row >= BLK, 1, jnp.where(a > 0, 1, 0))
    else:
        exists = 1
    ok = jnp.where(da >= 0, jnp.where(near, exists, 0), 0)
    dm = jnp.where(ok > 0, (da * dil).astype(F32), BIG)
    pairs = group // 2
    logits = []
    for kv in range(n_kv):
        ka, kb = _split_heads_k(k[:, (kv // 2) * BLK:(kv // 2 + 1) * BLK], kv % 2 == 1)
        qst = jnp.concatenate([get_q(kv * pairs + p_) for p_ in range(pairs)], axis=0)
        logits.append((lax.dot_general(ka, qst, NT, preferred_element_type=F32),
                       lax.dot_general(kb, qst, NT, preferred_element_type=F32)))
    accs, inv_all = [], []
    for kv in range(n_kv):
        s_even, s_odd = logits[kv]
        pts, invs = [], []
        for g in range(group):
            p_, odd = divmod(g, 2)
            h = kv * group + g
            s = (s_odd if odd else s_even)[:, p_ * BLK:(p_ + 1) * BLK] - slopes[h] * dm
            m = jnp.max(s, axis=0, keepdims=True)
            if has_sink:
                sink = sink_ref[h] * LOG2E
                m = jnp.maximum(m, sink)
            p = jnp.exp2(s - m)
            den = jnp.sum(p, axis=0, keepdims=True)
            if has_sink:
                den = den + jnp.exp2(sink - m)
            if with_lse:
                put_lse(h, m + jnp.log2(den))
            pts.append(p.astype(BF))
            invs.append(1.0 / den)
        accs.append(jnp.dot(vt[kv * HEAD_DIM:(kv + 1) * HEAD_DIM, :], jnp.concatenate(pts, axis=1),
                            preferred_element_type=F32))
        inv_all.append(invs)
    for kv in range(n_kv):
        acc, invs = accs[kv], inv_all[kv]
        for p_ in range(pairs):
            oe = acc[:, (2 * p_) * BLK:(2 * p_ + 1) * BLK] * invs[2 * p_]
            oo = acc[:, (2 * p_ + 1) * BLK:(2 * p_ + 2) * BLK] * invs[2 * p_ + 1]
            put_o(kv * pairs + p_, jnp.concatenate([oe, oo], axis=0).T)


def _swa_kernel(sink_ref, q_ref, kvp_ref, kvc_ref, o_ref, *, slopes, n_kv, group):
    dkv = n_kv * HEAD_DIM
    k = jnp.concatenate([kvp_ref[:, 0:dkv], kvc_ref[:, 0:dkv]], axis=0)
    v = jnp.concatenate([kvp_ref[:, dkv:2 * dkv], kvc_ref[:, dkv:2 * dkv]], axis=0)

    def put_o(tq, blk):
        o_ref[:, tq * BLK:(tq + 1) * BLK] = blk.astype(o_ref.dtype)

    _band_core(k, v, lambda tq: q_ref[:, tq * BLK:(tq + 1) * BLK], put_o, None, sink_ref,
               pl.program_id(1), slopes=slopes, n_kv=n_kv, group=group, dil=1, inclusive=False,
               has_prev=True)


def _swa_attention(qkv, sinks, *, batch, seq, n_heads, n_kv):
    t = qkv.shape[0]
    nq = seq // BLK
    dq, dkv = n_heads * HEAD_DIM, n_kv * HEAD_DIM
    assert dq % (2 * dkv) == 0 and nq > 1
    kvb = dq // (2 * dkv)
    return pl.pallas_call(
        functools.partial(_swa_kernel, slopes=_alibi_slopes(n_heads), n_kv=n_kv, group=n_heads // n_kv),
        out_shape=jax.ShapeDtypeStruct((t, dq), BF),
        grid=(batch, nq),
        in_specs=[pl.BlockSpec(memory_space=pltpu.SMEM),
                  pl.BlockSpec((BLK, dq), lambda b, i: (b * nq + i, 0)),
                  pl.BlockSpec((BLK, 2 * dkv), lambda b, i: (b * nq + jnp.maximum(i - 1, 0), kvb)),
                  pl.BlockSpec((BLK, 2 * dkv), lambda b, i: (b * nq + i, kvb))],
        out_specs=pl.BlockSpec((BLK, dq), lambda b, i: (b * nq + i, 0)),
        compiler_params=_cparams(("parallel", "parallel")),
        name="swa_attn",
    )(sinks.astype(F32), qkv, qkv, qkv)


def _dil_kernel(*refs, slopes, n_kv, group, dil, has_prev):
    refs = list(refs)
    q_refs = [refs.pop(0), refs.pop(0)]
    kv_refs = [refs.pop(0) for _ in range(2 if has_prev else 1)]
    o_ref, lse_ref, lse_scr = refs
    r = pl.program_id(2)
    rows = pl.ds(r, BLK, stride=dil) if dil > 1 else slice(None)
    ktiles = n_kv * HEAD_DIM // BLK

    def gather(ref, lo, hi):
        return jnp.concatenate([ref[cc, rows, :] for cc in range(lo, hi)], axis=1).astype(BF)

    k = jnp.concatenate([gather(ref, 0, ktiles) for ref in kv_refs], axis=0)
    v = jnp.concatenate([gather(ref, ktiles, 2 * ktiles) for ref in kv_refs], axis=0)
    per_ref = q_refs[0].shape[0]
    lse_scr[...] = jnp.zeros(lse_scr.shape, F32)

    def put_o(tq, blk):
        o_ref[tq, rows, :] = blk

    def put_lse(h, row):
        lse_scr[h:h + 1, :] = row

    _band_core(k, v, lambda tq: q_refs[tq // per_ref][tq % per_ref, rows, :].astype(BF), put_o, put_lse,
               None, pl.program_id(1), slopes=slopes, n_kv=n_kv, group=group, dil=dil, inclusive=True,
               has_prev=has_prev)
    lse_ref[rows, :] = lse_scr[...].T


def _dil_attention(proj_t, g, *, batch, seq, n_heads, n_kv, dil):
    t = proj_t.shape[1]
    cls = seq // dil
    assert seq % dil == 0 and cls % BLK == 0 and n_heads * HEAD_DIM == 8 * BLK and 2 * n_kv * HEAD_DIM == 4 * BLK
    n_a = cls // BLK
    rws = BLK * dil
    has_prev = n_a > 1
    cur = lambda blk: pl.BlockSpec((4, rws, BLK), lambda b, a, r: (blk, b * n_a + a, 0))
    prev = lambda blk: pl.BlockSpec((4, rws, BLK), lambda b, a, r: (blk, b * n_a + jnp.maximum(a - 1, 0), 0))
    specs = [cur(3 * g), cur(3 * g + 1)] + ([prev(3 * g + 2)] if has_prev else []) + [cur(3 * g + 2)]
    return pl.pallas_call(
        functools.partial(_dil_kernel, slopes=_alibi_slopes(n_heads), n_kv=n_kv, group=n_heads // n_kv,
                          dil=dil, has_prev=has_prev),
        out_shape=[jax.ShapeDtypeStruct((8, t, BLK), F32), jax.ShapeDtypeStruct((t, BLK), F32)],
        grid=(batch, n_a, dil),
        in_specs=specs,
        out_specs=[pl.BlockSpec((8, rws, BLK), lambda b, a, r: (0, b * n_a + a, 0)),
                   pl.BlockSpec((rws, BLK), lambda b, a, r: (b * n_a + a, 0))],
        scratch_shapes=[pltpu.VMEM((BLK, BLK), F32)],
        compiler_params=_cparams(("parallel", "parallel", "arbitrary")),
        name=f"dil_attn{g}",
    )(*([proj_t] * len(specs)))


def _dsa_kernel(q_ref, qi_ref, wq_ref, kv_ref, ki_ref, o_ref,
                vt_scr, ka_scr, kb_scr, qst_scr, qist_scr, sc_scr, bound_scr, m_scr, l_scr, acc_scr,
                *, slopes, topk, ch, seq, n_heads, n_idx, hd_dim, idx_dim):
    i = pl.program_id(1)
    t0 = i * BLK
    nch = (t0 + BLK - 1) // ch + 1

    @pl.when(i == 0)
    def _():
        vt_scr[...] = kv_ref[:, hd_dim:2 * hd_dim].T
        ka, kb = _split_heads_k(ki_ref[...], False)
        ka_scr[...] = ka
        kb_scr[...] = kb

    for h in range(n_heads):
        qst_scr[h * BLK:(h + 1) * BLK, :] = q_ref[:, h * hd_dim:(h + 1) * hd_dim]
    for p in range(n_idx // 2):
        qist_scr[p * BLK:(p + 1) * BLK, :] = qi_ref[:, p * BLK:(p + 1) * BLK]

    wscale = float(idx_dim ** -0.5) * float(n_idx ** -0.5)
    wt = wq_ref[...].astype(F32).T * wscale
    row = lax.broadcasted_iota(jnp.int32, (ch, BLK), 0)
    col = lax.broadcasted_iota(jnp.int32, (ch, BLK), 1)
    cr = col - row

    def p1(cc, carry):
        start = pl.multiple_of(cc * ch, ch)
        qi = qist_scr[...]
        rel_e = lax.dot_general(ka_scr[pl.ds(start, ch), :], qi, NT, preferred_element_type=F32)
        rel_o = lax.dot_general(kb_scr[pl.ds(start, ch), :], qi, NT, preferred_element_type=F32)
        score = jnp.zeros((ch, BLK), F32)
        for p in range(n_idx // 2):
            we = wt[idx_dim + 2 * p:idx_dim + 2 * p + 1, :]
            wo = wt[idx_dim + 2 * p + 1:idx_dim + 2 * p + 2, :]
            score = score + we * jnp.maximum(rel_e[:, p * BLK:(p + 1) * BLK], 0.0)
            score = score + wo * jnp.maximum(rel_o[:, p * BLK:(p + 1) * BLK], 0.0)
        sc_scr[pl.ds(start, ch), :] = jnp.where(cr + (t0 - start) >= 0, score, -jnp.inf)
        return carry

    lax.fori_loop(0, nch, p1, 0)

    tq = t0 + lax.broadcasted_iota(jnp.int32, (1, BLK), 1)
    kq = jnp.minimum(tq + 1, topk).astype(F32)

    def key_to_f32(key):
        bits = jnp.where(key < 0, key ^ jnp.int32(0x7FFFFFFF), key)
        return lax.bitcast_convert_type(bits, F32)

    def count(hit_fn):
        lanes = 64

        def body(cc, acc):
            start = pl.multiple_of(cc * ch, ch)
            hit = hit_fn(sc_scr[pl.ds(start, ch), :], start)
            for j in range(ch // lanes):
                acc = acc + hit[j * lanes:(j + 1) * lanes, :]
            return acc

        acc = lax.fori_loop(0, nch, body, jnp.zeros((lanes, BLK), F32))
        return jnp.sum(acc, axis=0, keepdims=True)

    def count_ge(cf):
        return count(lambda blk, start: jnp.where(blk >= cf, 1.0, 0.0))

    ans0 = jnp.where(count_ge(jnp.zeros((1, BLK), F32)) >= kq,
                     jnp.int32(0), jnp.int32(-2147483648))

    def radix(b, ans):
        cand = ans | lax.shift_left(jnp.int32(1), 30 - b)
        return jnp.where(count_ge(key_to_f32(cand)) >= kq, cand, ans)

    thr = key_to_f32(lax.fori_loop(0, 31, radix, ans0))

    bound_scr[...] = jnp.full(bound_scr.shape, seq, jnp.int32)

    @pl.when(jnp.max(count_ge(thr) - kq) > 0.0)
    def _():
        need = kq - count(lambda blk, start: jnp.where(blk > thr, 1.0, 0.0))

        def ties_below(x):
            return count(lambda blk, start: jnp.where(blk == thr, jnp.where(row + start < x, 1.0, 0.0), 0.0))

        nbits = (seq - 1).bit_length()

        def bsearch(b, x):
            cand = x | lax.shift_left(jnp.int32(1), nbits - 1 - b)
            return jnp.where(ties_below(cand) < need, cand, x)

        bound_scr[...] = lax.fori_loop(0, nbits, bsearch, jnp.zeros((1, BLK), jnp.int32)) + 1

    def p3(cc, carry):
        start = pl.multiple_of(cc * ch, ch)
        blk = sc_scr[pl.ds(start, ch), :]
        tie_ok = jnp.where(row + start < bound_scr[...], 1, 0)
        keep = jnp.where(blk > thr, 1, jnp.where(blk == thr, tie_ok, 0))
        sc_scr[pl.ds(start, ch), :] = jnp.where(keep > 0, (cr + (t0 - start)).astype(F32), BIG)
        return carry

    lax.fori_loop(0, nch, p3, 0)

    m_scr[...] = jnp.full(m_scr.shape, NEG, F32)
    l_scr[...] = jnp.zeros(l_scr.shape, F32)
    acc_scr[...] = jnp.zeros(acc_scr.shape, F32)

    def p4(cc, carry):
        start = pl.multiple_of(cc * ch, ch)
        s_all = lax.dot_general(kv_ref[pl.ds(start, ch), 0:hd_dim], qst_scr[...], NT,
                                preferred_element_type=F32)
        dm = sc_scr[pl.ds(start, ch), :]
        pts, alphas = [], []
        for h in range(n_heads):
            hs = slice(h * BLK, (h + 1) * BLK)
            s = s_all[:, hs] - slopes[h] * dm
            m_old = m_scr[:, hs]
            m_new = jnp.maximum(m_old, jnp.max(s, axis=0, keepdims=True))
            alpha = jnp.exp2(m_old - m_new)
            p = jnp.exp2(s - m_new)
            l_scr[:, hs] = alpha * l_scr[:, hs] + jnp.sum(p, axis=0, keepdims=True)
            m_scr[:, hs] = m_new
            pts.append(p.astype(BF))
            alphas.append(alpha)
        pv = jnp.dot(vt_scr[:, pl.ds(start, ch)], jnp.concatenate(pts, axis=1),
                     preferred_element_type=F32)
        acc_scr[...] = jnp.concatenate(alphas, axis=1) * acc_scr[...] + pv
        return carry

    lax.fori_loop(0, nch, p4, 0)
    for h in range(n_heads):
        hs = slice(h * BLK, (h + 1) * BLK)
        o_ref[:, h * hd_dim:(h + 1) * hd_dim] = (acc_scr[:, hs] * (1.0 / l_scr[:, hs])).T.astype(o_ref.dtype)


def _dsa_attention(qq, proj, kv, *, batch, seq, n_heads, hd_dim, n_idx, idx_dim, kidx_block):
    t = qq.shape[0]
    nq = seq // BLK
    dq = n_heads * hd_dim
    di = n_idx * idx_dim
    assert dq % di == 0 and hd_dim == BLK and idx_dim == HEAD_DIM
    topk = min(TOPK_MAX, seq // 4)
    ch = min(512, seq)
    return pl.pallas_call(
        functools.partial(_dsa_kernel, slopes=_alibi_slopes(n_heads), topk=topk, ch=ch, seq=seq,
                          n_heads=n_heads, n_idx=n_idx, hd_dim=hd_dim, idx_dim=idx_dim),
        out_shape=jax.ShapeDtypeStruct((t, dq), BF),
        grid=(batch, nq),
        in_specs=[pl.BlockSpec((BLK, dq), lambda b, i: (b * nq + i, 0)),
                  pl.BlockSpec((BLK, di), lambda b, i: (b * nq + i, dq // di)),
                  pl.BlockSpec((BLK, BLK), lambda b, i: (b * nq + i, kidx_block)),
                  pl.BlockSpec((seq, 2 * hd_dim), lambda b, i: (b, 0)),
                  pl.BlockSpec((seq, BLK), lambda b, i: (b, kidx_block))],
        out_specs=pl.BlockSpec((BLK, dq), lambda b, i: (b * nq + i, 0)),
        scratch_shapes=[pltpu.VMEM((hd_dim, seq), BF),
                        pltpu.VMEM((seq, BLK), BF),
                        pltpu.VMEM((seq, BLK), BF),
                        pltpu.VMEM((n_heads * BLK, hd_dim), BF),
                        pltpu.VMEM((n_idx // 2 * BLK, BLK), BF),
                        pltpu.VMEM((seq, BLK), F32),
                        pltpu.VMEM((1, BLK), jnp.int32),
                        pltpu.VMEM((1, n_heads * BLK), F32),
                        pltpu.VMEM((1, n_heads * BLK), F32),
                        pltpu.VMEM((hd_dim, n_heads * BLK), F32)],
        compiler_params=_cparams(("parallel", "arbitrary")),
        name="dsa_attn",
    )(qq, qq, proj, kv, proj)


def _mixer_swa(xf, pro, gate, nxt, w_in, w_out, j, qn_g, kn_g, sinks, *, batch, seq):
    d = xf.shape[1]
    n_heads = d // HEAD_DIM
    n_kv = n_heads // 8
    dq, dkv = n_heads * HEAD_DIM, n_kv * HEAD_DIM
    gs = np.concatenate([np.full(dq + dkv, HEAD_DIM), np.zeros(dkv)]).astype(np.int64)
    gain = jnp.concatenate([jnp.tile(qn_g * (HEAD_DIM ** -0.5 * LOG2E), n_heads), jnp.tile(kn_g, n_kv),
                            jnp.ones((dkv,), F32)])
    qkv = _matmul(xf, w_in, j, seq=seq, tm=1024, tn=512, out_dtype=BF, prologue=pro,
                  gnorm=(gs, gain), name="swa_in")
    o = _swa_attention(qkv, sinks, batch=batch, seq=seq, n_heads=n_heads, n_kv=n_kv)
    return _out_proj(o, w_out, j, xf, gate, nxt, seq=seq, tm=512, name="swa_out")


def _mixer_dsa(xf, pro, gate, nxt, w_in, qlat_g, kvlat_g, w_uq, w_ukv, w_idx_q, qn_g, kn_g, w_out, j,
               *, batch, seq):
    qrank, kvrank = qlat_g.shape[0], kvlat_g.shape[0]
    hd_dim = kn_g.shape[0]
    n_heads = w_uq.shape[1] // hd_dim
    n_in = w_in.shape[1]
    n_idx = 16
    idx_dim = n_in - qrank - kvrank - n_idx
    assert idx_dim + n_idx <= 128 and (qrank + kvrank) % 128 == 0
    pad = qrank + kvrank + 128 - n_in
    w_in_p = jnp.pad(w_in, ((0, 0), (0, pad))).astype(BF)[None]
    gs = np.concatenate([np.full(qrank, qrank), np.full(kvrank, kvrank), np.zeros(128)]).astype(np.int64)
    gain = jnp.concatenate([qlat_g, kvlat_g, jnp.ones((128,), F32)])
    proj = _matmul(xf, w_in_p, 0, seq=seq, tm=512, tn=qrank + kvrank + 128, out_dtype=BF, prologue=pro,
                   gnorm=(gs, gain), name="dsa_in")
    w_q = jnp.concatenate([w_uq, w_idx_q], axis=1).astype(BF)[None]
    nq_cols, ni_cols = w_uq.shape[1], w_idx_q.shape[1]
    gs_q = np.concatenate([np.full(nq_cols, hd_dim), np.zeros(ni_cols)]).astype(np.int64)
    gain_q = jnp.concatenate([jnp.tile(qn_g * (hd_dim ** -0.5 * LOG2E), n_heads), jnp.ones((ni_cols,), F32)])
    qq = _matmul(proj, w_q, 0, seq=seq, tm=2048, tn=256, out_dtype=BF, lhs_cols=(0, qrank),
                 gnorm=(gs_q, gain_q), name="dsa_q")
    gs_kv = np.concatenate([np.full(hd_dim, hd_dim), np.zeros(hd_dim)]).astype(np.int64)
    gain_kv = jnp.concatenate([kn_g, jnp.ones((hd_dim,), F32)])
    kv = _matmul(proj, w_ukv[None], 0, seq=seq, tm=1024, tn=2 * hd_dim, out_dtype=BF,
                 lhs_cols=(qrank // kvrank, kvrank), gnorm=(gs_kv, gain_kv), name="dsa_kv")
    o = _dsa_attention(qq, proj, kv, batch=batch, seq=seq, n_heads=n_heads, hd_dim=hd_dim,
                       n_idx=n_idx, idx_dim=idx_dim, kidx_block=(qrank + kvrank) // 128)
    return _out_proj(o, w_out, j, xf, gate, nxt, seq=seq, tm=512, name="dsa_out")


def _mixer_dil(xf, pro, gate, nxt, w_in, w_out, j, qn_g, kn_g, *, batch, seq):
    n_groups = len(DIL_PATTERNS)
    n_heads = w_out.shape[1] // HEAD_DIM
    n_kv = n_heads // 4
    dq, dkv = n_heads * HEAD_DIM, n_kv * HEAD_DIM
    gcols = dq + 2 * dkv
    gs = np.tile(np.concatenate([np.full(dq + dkv, HEAD_DIM), np.zeros(dkv)]), n_groups).astype(np.int64)
    gain = jnp.concatenate([
        jnp.concatenate([jnp.tile(qn_g[g] * (HEAD_DIM ** -0.5 * LOG2E), n_heads), jnp.tile(kn_g[g], n_kv),
                         jnp.ones((dkv,), F32)]) for g in range(n_groups)])
    assert gcols == 12 * BLK
    proj_t = _matmul(xf, w_in, j, seq=seq, tm=1024, tn=512, out_dtype=F32, prologue=pro,
                     gnorm=(gs, gain), tile_major=True, name="dil_in")
    outs, lses = [], []
    for g, (win, dil) in enumerate(DIL_PATTERNS):
        assert win == BLK * dil
        o, lse = _dil_attention(proj_t, g, batch=batch, seq=seq, n_heads=n_heads, n_kv=n_kv, dil=dil)
        outs.append(o)
        lses.append(lse)
    return _out_proj(None, w_out, j, xf, gate, nxt, seq=seq, tm=512, mix=(outs, lses), name="dil_out")


def _mlp(xf, h, gate, w1, w2, i, *, seq):
    u = _matmul(h, w1, i, seq=seq, tm=2048, tn=512, out_dtype=BF, relu2=True, name="mlp_up")
    return _matmul_ktiled_resid(u, w2, i, xf, gate, seq=seq, tm=1024, tn=1024, tk=2048,
                                name="mlp_down")


def kernel(x, c, ada_w, ada_b, norm1_g, norm2_g, mlp_w1, mlp_w2, swa_w_in, swa_qn_g, swa_kn_g, swa_sinks, swa_w_out, dsa_w_in, dsa_qlat_g, dsa_kvlat_g, dsa_w_uq, dsa_w_ukv, dsa_w_idx_q, dsa_qn_g, dsa_kn_g, dsa_w_out, dil_w_in, dil_qn_g, dil_kn_g, dil_w_out):
    batch, seq, d = x.shape
    depth = ada_w.shape[0]
    mod = _adaln_mod(c, ada_w, ada_b)
    xf = x.reshape(batch * seq, d)
    mlp_w2_bf = mlp_w2.astype(BF)
    swa_w_in_bf, dil_w_in_bf = swa_w_in.astype(BF), dil_w_in.astype(BF)
    swa_w_out_bf, dsa_w_out_bf, dil_w_out_bf = (w.astype(BF) for w in (swa_w_out, dsa_w_out, dil_w_out))
    for i in range(depth):
        m = mod[i].reshape(batch, 1, 6 * d)
        sh1, sc1, g1, sh2, sc2, g2 = [m[:, :, q * d:(q + 1) * d] for q in range(6)]
        pro, nxt = (norm1_g[i], sh1, sc1), (norm2_g[i], sh2, sc2)
        kind, j = i % 3, i // 3
        if kind == 0:
            xf, h2 = _mixer_swa(xf, pro, g1, nxt, swa_w_in_bf, swa_w_out_bf, j, swa_qn_g[j], swa_kn_g[j],
                                swa_sinks[j], batch=batch, seq=seq)
        elif kind == 1:
            xf, h2 = _mixer_dsa(xf, pro, g1, nxt, dsa_w_in[j], dsa_qlat_g[j], dsa_kvlat_g[j], dsa_w_uq[j],
                                dsa_w_ukv[j], dsa_w_idx_q[j], dsa_qn_g[j], dsa_kn_g[j], dsa_w_out_bf, j,
                                batch=batch, seq=seq)
        else:
            xf, h2 = _mixer_dil(xf, pro, g1, nxt, dil_w_in_bf, dil_w_out_bf, j, dil_qn_g[j], dil_kn_g[j],
                                batch=batch, seq=seq)
        xf = _mlp(xf, h2, g2, mlp_w1, mlp_w2_bf, i, seq=seq)
    return xf.reshape(batch, seq, d)
```
